```python
import math
import jax, jax.numpy as jnp
from jax import lax
import numpy as np


D_MODEL = 1024
BATCH = 2
SEQ = 8192
DEPTH = 4
DEC_BATCH = 32
DEC_SEQ = 4
PAST_LEN = 8192
PAGE_SIZE = 128

HEAD_DIM = 64
N_HEADS_ATT = 8
N_HEADS_MLSTM = 8
D_ATT = N_HEADS_ATT * HEAD_DIM
D_MLSTM = N_HEADS_MLSTM * HEAD_DIM
MIX_WIDTH = D_ATT + D_MLSTM
D_IN_PROJ = 3 * D_ATT + 4 * D_MLSTM + 2 * N_HEADS_MLSTM
MOBA_BLOCK = 256
MOBA_TOPK = 3
MOBA_QBLOCK = 32
MLSTM_CHUNK = 64
CONV_W = 4
N_EXPERTS = 32
TOP_K = 4
D_FF = 1024
SWIGLU_ALPHA = 1.702
SWIGLU_LIMIT = 7.0
MOE_BLOCK = 128
N_ADA = 6
EPS = 1e-6
NEG = -1e30

kernel_name = "hymba_moba_mlstm_moe_adaln_step"


def rms_norm(x, g):
    xf = x.astype(jnp.float32)
    y = xf * lax.rsqrt(jnp.mean(xf * xf, axis=-1, keepdims=True) + EPS)
    return (y * g.astype(jnp.float32)).astype(x.dtype)


def causal_conv(x, buf, w, b):
    T = x.shape[1]
    xp = jnp.concatenate([buf.astype(x.dtype), x], axis=1)
    y = b + sum(xp[:, j:j + T] * w[j] for j in range(CONV_W))
    return y, xp[:, -(CONV_W - 1):]


def moba_attention(q, k, v, q_pos0):
    B, H, T, d = q.shape
    Lk = k.shape[2]
    NB = max(-(-Lk // MOBA_BLOCK), MOBA_TOPK)
    pad = NB * MOBA_BLOCK - Lk
    k = jnp.pad(k, ((0, 0), (0, 0), (0, pad), (0, 0)))
    v = jnp.pad(v, ((0, 0), (0, 0), (0, pad), (0, 0)))
    kb = k.reshape(B, H, NB, MOBA_BLOCK, d)
    vb = v.reshape(B, H, NB, MOBA_BLOCK, d)
    k_mean = kb.astype(jnp.float32).mean(axis=3)
    QB = math.gcd(T, MOBA_QBLOCK)
    NQ = T // QB
    qs = jnp.moveaxis(q.reshape(B, H, NQ, QB, d), 2, 0)
    pos = (q_pos0 + jnp.arange(T, dtype=jnp.int32)).reshape(NQ, QB)
    bi = jnp.arange(B)[:, None, None, None]
    hi = jnp.arange(H)[None, :, None, None]
    blk_ids = jnp.arange(NB, dtype=jnp.int32)
    offs = jnp.arange(MOBA_BLOCK, dtype=jnp.int32)
    rank = jnp.arange(MOBA_TOPK, dtype=jnp.int32)
    scale = d ** -0.5

    def attend(args):
        qc, pc = args
        own = pc // MOBA_BLOCK
        gate = jnp.einsum('bhqd,bhnd->bhqn', qc.astype(jnp.float32), k_mean)
        gate = jnp.where(blk_ids[None, :] < own[:, None], gate, NEG)
        _, sel = lax.top_k(gate, MOBA_TOPK)
        blocks = jnp.concatenate([sel.astype(jnp.int32), jnp.broadcast_to(own[None, None, :, None], (B, H, QB, 1))], axis=-1)
        blk_ok = jnp.concatenate([rank[None, :] < own[:, None], jnp.ones((QB, 1), bool)], axis=-1)
        kg = kb[bi, hi, blocks]
        vg = vb[bi, hi, blocks]
        key_pos = blocks[..., None] * MOBA_BLOCK + offs
        mask = blk_ok[None, None, :, :, None] & (key_pos <= pc[None, None, :, None, None])
        s = jnp.einsum('bhqd,bhqjkd->bhqjk', qc, kg).astype(jnp.float32) * scale
        s = jnp.where(mask, s, NEG)
        p = jax.nn.softmax(s.reshape(B, H, QB, -1), axis=-1).reshape(s.shape).astype(vg.dtype)
        return jnp.einsum('bhqjk,bhqjkd->bhqd', p, vg)

    out = lax.map(attend, (qs, pos))
    return jnp.moveaxis(out, 0, 2).reshape(B, H, T, d)


def mlstm_chunkwise(q, k, v, log_i, log_f, c0, n0, m0):
    B, H, T, d = q.shape
    L = math.gcd(T, MLSTM_CHUNK)
    NC = T // L

    def to_chunks(a):
        return jnp.moveaxis(a.reshape(B, H, NC, L, *a.shape[3:]), 2, 0)

    causal = jnp.tril(jnp.ones((L, L), bool))

    def step(carry, inp):
        C, n, m = carry
        qc, kc, vc, ic, fc = inp
        b = jnp.cumsum(fc, axis=-1)
        log_d = jnp.where(causal, b[..., :, None] - b[..., None, :] + ic[..., None, :], -jnp.inf)
        log_inter = m[..., None] + b
        m_t = jnp.maximum(log_inter, log_d.max(axis=-1))
        dw = jnp.exp(log_d - m_t[..., None])
        w_inter = jnp.exp(log_inter - m_t)
        s = jnp.einsum('bhtd,bhsd->bhts', qc, kc) * dw
        num = jnp.einsum('bhts,bhsv->bhtv', s, vc) + w_inter[..., None] * jnp.einsum('bhtd,bhdv->bhtv', qc, C)
        den = s.sum(axis=-1) + w_inter * jnp.einsum('bhtd,bhd->bht', qc, n)
        h = num / jnp.maximum(jnp.abs(den), jnp.exp(-m_t))[..., None]
        m_new = m_t[..., -1]
        w_state = jnp.exp(b[..., -1:] - b + ic - m_new[..., None])
        decay = jnp.exp(log_inter[..., -1] - m_new)
        C = decay[..., None, None] * C + jnp.einsum('bhs,bhsd,bhsv->bhdv', w_state, kc, vc)
        n = decay[..., None] * n + jnp.einsum('bhs,bhsd->bhd', w_state, kc)
        return (C, n, m_new), h

    (C, n, m), hs = lax.scan(step, (c0, n0, m0), tuple(to_chunks(a) for a in (q, k, v, log_i, log_f)))
    h = jnp.moveaxis(hs, 0, 2).reshape(B, H, T, d)
    return h, C, n, m


def moe_ffn(x, w_router, b_router, w_gu, b_gu, w_down, b_down):
    N, D = x.shape
    logits = (x @ w_router + b_router).astype(jnp.float32)
    top_val, top_idx = lax.top_k(logits, TOP_K)
    gate_w = jax.nn.softmax(top_val, axis=-1)
    NA = N * TOP_K
    e_flat = top_idx.reshape(-1).astype(jnp.int32)
    order = jnp.argsort(e_flat)
    e_sorted = e_flat[order]
    tok_sorted = (order // TOP_K).astype(jnp.int32)
    w_sorted = gate_w.reshape(-1)[order]
    counts = jnp.zeros((N_EXPERTS,), jnp.int32).at[e_flat].add(1)
    padded = ((counts + MOE_BLOCK - 1) // MOE_BLOCK) * MOE_BLOCK
    pad_end = jnp.cumsum(padded)
    pad_start = pad_end - padded
    grp_start = jnp.cumsum(counts) - counts
    dest = pad_start[e_sorted] + jnp.arange(NA, dtype=jnp.int32) - grp_start[e_sorted]
    n_blocks = -(-(NA + N_EXPERTS * (MOE_BLOCK - 1)) // MOE_BLOCK)
    M = n_blocks * MOE_BLOCK
    row_tok = jnp.full((M,), N, jnp.int32).at[dest].set(tok_sorted)
    row_w = jnp.zeros((M,), jnp.float32).at[dest].set(w_sorted)
    blk_exp = jnp.minimum(jnp.searchsorted(pad_end, jnp.arange(n_blocks, dtype=jnp.int32) * MOE_BLOCK, side='right'), N_EXPERTS - 1)
    x_pad = jnp.concatenate([x, jnp.zeros((1, D), x.dtype)], axis=0)
    xb = x_pad[row_tok].reshape(n_blocks, MOE_BLOCK, D)

    def expert_block(args):
        xe, e = args
        gu = xe @ w_gu[e] + b_gu[e]
        g, u = gu[:, :D_FF], gu[:, D_FF:]
        g = jnp.minimum(g, SWIGLU_LIMIT)
        u = jnp.clip(u, -SWIGLU_LIMIT, SWIGLU_LIMIT)
        act = (u + 1) * g * jax.nn.sigmoid(g * SWIGLU_ALPHA)
        return act @ w_down[e] + b_down[e]

    yb = lax.map(expert_block, (xb, blk_exp)).reshape(M, D)
    out = jnp.zeros((N + 1, D), x.dtype).at[row_tok].add(yb * row_w[:, None].astype(x.dtype))
    return out[:N]


def decoder_layer(x, c, k_past, v_past, conv_buf, c0, n0, m0, q_pos0,
                  w_ada, b_ada, g_mix, w_in, b_gate, conv_w, conv_b, g_head, w_out,
                  g_ffn, w_router, b_router, w_gu, b_gu, w_down, b_down):
    B, T, D = x.shape
    mod = jax.nn.silu(c) @ w_ada + b_ada
    sh1, sc1, gt1, sh2, sc2, gt2 = jnp.split(mod[:, None, :], N_ADA, axis=-1)
    h = rms_norm(x, g_mix) * (1 + sc1) + sh1
    proj = h @ w_in
    sizes = [D_ATT, D_ATT, D_ATT, D_MLSTM, D_MLSTM, D_MLSTM, D_MLSTM]
    cuts = [int(s) for s in np.cumsum(sizes)]
    qa, ka, va, qm, km, vm, om, gates = jnp.split(proj, cuts, axis=-1)

    def heads(a):
        return a.reshape(B, T, -1, HEAD_DIM).transpose(0, 2, 1, 3)

    k_h, v_h = heads(ka), heads(va)
    k_all = k_h if k_past is None else jnp.concatenate([k_past.astype(k_h.dtype), k_h], axis=2)
    v_all = v_h if v_past is None else jnp.concatenate([v_past.astype(v_h.dtype), v_h], axis=2)
    att = moba_attention(heads(qa), k_all, v_all, q_pos0)
    att = att.transpose(0, 2, 1, 3).reshape(B, T, D_ATT)

    qk, conv_new = causal_conv(jnp.concatenate([qm, km], axis=-1), conv_buf, conv_w, conv_b)
    qk = jax.nn.silu(qk)
    qm_c, km_c = jnp.split(qk, 2, axis=-1)
    g = (gates + b_gate).astype(jnp.float32)
    log_i = g[..., :N_HEADS_MLSTM].transpose(0, 2, 1)
    log_f = jax.nn.log_sigmoid(g[..., N_HEADS_MLSTM:]).transpose(0, 2, 1)
    f32 = jnp.float32
    hm, C, n, m = mlstm_chunkwise(heads(qm_c).astype(f32), heads(km_c).astype(f32) * (HEAD_DIM ** -0.5),
                                  heads(vm).astype(f32), log_i, log_f,
                                  c0.astype(f32), n0.astype(f32), m0.astype(f32))
    hm = hm.transpose(0, 2, 1, 3)
    hm = hm * lax.rsqrt(jnp.mean(hm * hm, axis=-1, keepdims=True) + EPS) * g_head.astype(f32).reshape(N_HEADS_MLSTM, HEAD_DIM)
    hm = (hm.reshape(B, T, D_MLSTM) * jax.nn.sigmoid(om.astype(f32))).astype(x.dtype)

    mix = jnp.concatenate([att.astype(x.dtype), hm], axis=-1) @ w_out
    x = x + gt1 * mix

    h2 = rms_norm(x, g_ffn) * (1 + sc2) + sh2
    ff = moe_ffn(h2.reshape(B * T, D), w_router, b_router, w_gu, b_gu, w_down, b_down).reshape(B, T, D)
    x = x + gt2 * ff
    k_rows = ka.reshape(B, T, N_HEADS_ATT, HEAD_DIM)
    v_rows = va.reshape(B, T, N_HEADS_ATT, HEAD_DIM)
    return x, k_rows, v_rows, C.astype(x.dtype), n.astype(x.dtype), m.astype(x.dtype), conv_new


def setup_inputs(seed: int = 0) -> dict:
    key = jax.random.key(seed)
    ks = jax.random.split(key, 32)
    f32 = jnp.float32
    n_pages = PAST_LEN // PAGE_SIZE
    n_pool = (5 * DEC_BATCH * n_pages + 3) // 4

    def nrm(k, shape, s=1.0):
        return s * jax.random.normal(k, shape, f32)

    page_table = jax.random.permutation(ks[7], n_pool)[:DEC_BATCH * n_pages].reshape(DEC_BATCH, n_pages).astype(jnp.int32)
    b_gate = jnp.concatenate([
        nrm(ks[14], (DEPTH, N_HEADS_MLSTM), 0.1),
        jnp.linspace(3.0, 6.0, N_HEADS_MLSTM, dtype=f32)[None, :] + nrm(ks[15], (DEPTH, N_HEADS_MLSTM), 0.1)], axis=-1)
    return {
        "x_prompt": nrm(ks[0], (BATCH, SEQ, D_MODEL)),
        "x_sample": nrm(ks[1], (DEC_BATCH, DEC_SEQ, D_MODEL)),
        "c_prompt": nrm(ks[2], (BATCH, D_MODEL)),
        "c_sample": nrm(ks[3], (DEC_BATCH, D_MODEL)),
        "cache_k": nrm(ks[4], (DEPTH, n_pool, PAGE_SIZE, N_HEADS_ATT, HEAD_DIM)),
        "cache_v": nrm(ks[5], (DEPTH, n_pool, PAGE_SIZE, N_HEADS_ATT, HEAD_DIM)),
        "page_table": page_table,
        "state_c": nrm(ks[6], (DEPTH, DEC_BATCH, N_HEADS_MLSTM, HEAD_DIM, HEAD_DIM), 0.3),
        "state_n": nrm(ks[8], (DEPTH, DEC_BATCH, N_HEADS_MLSTM, HEAD_DIM), 0.3),
        "state_m": nrm(ks[9], (DEPTH, DEC_BATCH, N_HEADS_MLSTM), 0.5),
        "state_conv": nrm(ks[10], (DEPTH, DEC_BATCH, CONV_W - 1, 2 * D_MLSTM)),
        "w_ada": nrm(ks[11], (DEPTH, D_MODEL, N_ADA * D_MODEL), 0.5 * D_MODEL ** -0.5),
        "b_ada": nrm(ks[12], (DEPTH, N_ADA * D_MODEL), 0.02),
        "g_mix": 1.0 + nrm(ks[13], (DEPTH, D_MODEL), 0.02),
        "w_in": nrm(ks[16], (DEPTH, D_MODEL, D_IN_PROJ), D_MODEL ** -0.5),
        "b_gate": b_gate,
        "conv_w": nrm(ks[17], (DEPTH, CONV_W, 2 * D_MLSTM), CONV_W ** -0.5),
        "conv_b": nrm(ks[18], (DEPTH, 2 * D_MLSTM), 0.02),
        "g_head": 1.0 + nrm(ks[19], (DEPTH, D_MLSTM), 0.02),
        "w_out": nrm(ks[20], (DEPTH, MIX_WIDTH, D_MODEL), MIX_WIDTH ** -0.5),
        "g_ffn": 1.0 + nrm(ks[21], (DEPTH, D_MODEL), 0.02),
        "w_router": nrm(ks[22], (DEPTH, D_MODEL, N_EXPERTS), D_MODEL ** -0.5),
        "b_router": nrm(ks[23], (DEPTH, N_EXPERTS), 0.01),
        "w_gu": nrm(ks[24], (DEPTH, N_EXPERTS, D_MODEL, 2 * D_FF), D_MODEL ** -0.5),
        "b_gu": nrm(ks[25], (DEPTH, N_EXPERTS, 2 * D_FF), 0.02),
        "w_down": nrm(ks[26], (DEPTH, N_EXPERTS, D_FF, D_MODEL), D_FF ** -0.5),
        "b_down": nrm(ks[27], (DEPTH, N_EXPERTS, D_MODEL), 0.02),
        "g_final": 1.0 + nrm(ks[28], (D_MODEL,), 0.02),
    }


def reference(x_prompt, x_sample, c_prompt, c_sample, cache_k, cache_v, page_table,
              state_c, state_n, state_m, state_conv,
              w_ada, b_ada, g_mix, w_in, b_gate, conv_w, conv_b, g_head, w_out,
              g_ffn, w_router, b_router, w_gu, b_gu, w_down, b_down, g_final):
    dt = x_prompt.dtype
    xp, xs = x_prompt, x_sample
    nkp, nvp, ncp, nnp, nmp, nconvp = [], [], [], [], [], []
    nks, nvs, ncs, nns, nms, nconvs = [], [], [], [], [], []
    past_len = page_table.shape[1] * cache_k.shape[2]
    for l in range(DEPTH):
        lp = (w_ada[l], b_ada[l], g_mix[l], w_in[l], b_gate[l], conv_w[l], conv_b[l], g_head[l], w_out[l],
              g_ffn[l], w_router[l], b_router[l], w_gu[l], b_gu[l], w_down[l], b_down[l])
        B = xp.shape[0]
        xp, kr, vr, C, n, m, cv = decoder_layer(
            xp, c_prompt, None, None,
            jnp.zeros((B, CONV_W - 1, 2 * D_MLSTM), dt),
            jnp.zeros((B, N_HEADS_MLSTM, HEAD_DIM, HEAD_DIM), dt),
            jnp.zeros((B, N_HEADS_MLSTM, HEAD_DIM), dt),
            jnp.zeros((B, N_HEADS_MLSTM), dt), 0, *lp)
        nkp.append(kr); nvp.append(vr); ncp.append(C); nnp.append(n); nmp.append(m); nconvp.append(cv)
        DB = xs.shape[0]
        k_past = cache_k[l][page_table].reshape(DB, past_len, N_HEADS_ATT, HEAD_DIM).transpose(0, 2, 1, 3)
        v_past = cache_v[l][page_table].reshape(DB, past_len, N_HEADS_ATT, HEAD_DIM).transpose(0, 2, 1, 3)
        xs, kr, vr, C, n, m, cv = decoder_layer(
            xs, c_sample, k_past, v_past, state_conv[l], state_c[l], state_n[l], state_m[l], past_len, *lp)
        nks.append(kr); nvs.append(vr); ncs.append(C); nns.append(n); nms.append(m); nconvs.append(cv)
    y_prompt = rms_norm(xp, g_final)
    y_sample = rms_norm(xs, g_final)
    return (y_prompt, y_sample,
            jnp.stack(nkp), jnp.stack(nvp), jnp.stack(ncp), jnp.stack(nnp), jnp.stack(nmp), jnp.stack(nconvp),
            jnp.stack(nks), jnp.stack(nvs), jnp.stack(ncs), jnp.stack(nns), jnp.stack(nms), jnp.stack(nconvs))
```

```python
import functools

import jax
import jax.numpy as jnp
from jax import lax
from jax.experimental import pallas as pl
from jax.experimental.pallas import tpu as pltpu

F32 = jnp.float32
BF16 = jnp.bfloat16
I32 = jnp.int32

HEAD_DIM = 64
N_HEADS = 8
D_GROUP = N_HEADS * HEAD_DIM
MOBA_BLOCK = 256
MOBA_TOPK = 3
CONV_W = 4
N_EXPERTS = 32
TOP_K = 4
SWIGLU_ALPHA = 1.702
SWIGLU_LIMIT = 7.0
N_ADA = 6
EPS = 1e-6
NEG = -1e30
LANES = 128
GATE_PAD = LANES
MOE_BM = 512
VMEM_LIMIT = 56 * 1024 * 1024


def _cparams(sem, vmem=None):
    return pltpu.CompilerParams(dimension_semantics=sem, vmem_limit_bytes=vmem)


def _split2(x):
    hi = x.astype(BF16)
    lo = (x - hi.astype(F32)).astype(BF16)
    return hi, lo


def _split3(x):
    hi = x.astype(BF16)
    r = x - hi.astype(F32)
    mid = r.astype(BF16)
    lo = (r - mid.astype(F32)).astype(BF16)
    return hi, mid, lo


_NT = (((1,), (1,)), ((), ()))
_TN = (((0,), (0,)), ((), ()))


def _dot(a, b):
    return jnp.dot(a, b, preferred_element_type=F32)


def _dot_nt(a, b):
    return lax.dot_general(a, b, _NT, preferred_element_type=F32)


def _ada_kernel(c_ref, w_ref, b_ref, o_ref):
    c = c_ref[...]
    s = (c * jax.nn.sigmoid(c)).astype(BF16)
    o_ref[...] = _dot(s, w_ref[...].astype(BF16)) + b_ref[...]


def ada_mod(c_all, w_ada, b_ada):
    depth, d, nd = w_ada.shape
    r = c_all.shape[0]
    nj = nd // d
    return pl.pallas_call(
        _ada_kernel,
        grid=(depth, nj),
        in_specs=[pl.BlockSpec((r, d), lambda l, j: (0, 0)),
                  pl.BlockSpec((None, d, d), lambda l, j: (l, 0, j)),
                  pl.BlockSpec((None, 1, d), lambda l, j: (l, 0, j))],
        out_specs=pl.BlockSpec((None, r, d), lambda l, j: (l, 0, j)),
        out_shape=jax.ShapeDtypeStruct((depth, r, nd), F32),
        compiler_params=_cparams(("arbitrary", "arbitrary")),
        name="ada_mod",
    )(c_all, w_ada, b_ada.reshape(depth, 1, nd))


def _inproj_kernel(x_ref, g_ref, sc_ref, sh_ref, w_ref,
                   qa_ref, ka_ref, va_ref, kb_ref, vb_ref, qk_ref, vm_ref, om_ref, gt_ref, *rest,
                   with_kmean):
    x = x_ref[...]
    ms = jnp.mean(x * x, axis=-1, keepdims=True)
    y = x * lax.rsqrt(ms + EPS) * g_ref[...]
    h = (y * (1.0 + sc_ref[...]) + sh_ref[...]).astype(BF16)
    dg = D_GROUP

    def seg(a, b):
        return _dot(h, w_ref[:, a:b])

    qa_ref[...] = seg(0, dg)
    ka = seg(dg, 2 * dg)
    ka_ref[...] = ka
    kb_ref[...] = ka.astype(BF16)
    va = seg(2 * dg, 3 * dg)
    va_ref[...] = va
    vb_ref[...] = va.astype(BF16)
    qk_ref[...] = seg(3 * dg, 5 * dg)
    vm_ref[...] = seg(5 * dg, 6 * dg).astype(BF16)
    om_ref[...] = seg(6 * dg, 7 * dg)
    gt_ref[...] = seg(7 * dg, 7 * dg + GATE_PAD)
    if with_kmean:
        km_ref = rest[0]
        nblk = ka.shape[0] // MOBA_BLOCK
        rows = [jnp.sum(ka[i * MOBA_BLOCK:(i + 1) * MOBA_BLOCK], axis=0, keepdims=True) for i in range(nblk)]
        km_ref[...] = jnp.concatenate(rows, axis=0) * (1.0 / MOBA_BLOCK)


def in_proj(x, mod, g, w_pad, *, tm, tiles_per_group, with_kmean):
    n, d = x.shape
    rm = mod.shape[1]
    nw = w_pad.shape[1]
    dg = D_GROUP
    grid = (n // tm,)
    row = lambda i: (i, 0)
    modspec = lambda c: pl.BlockSpec((None, rm, d), lambda i, c=c: (i // tiles_per_group, 0, c))
    out_shapes = [jax.ShapeDtypeStruct((n, dg), F32)] * 3 + [jax.ShapeDtypeStruct((n, dg), BF16)] * 2 + [
        jax.ShapeDtypeStruct((n, 2 * dg), F32), jax.ShapeDtypeStruct((n, dg), BF16),
        jax.ShapeDtypeStruct((n, dg), F32), jax.ShapeDtypeStruct((n, GATE_PAD), F32)]
    out_specs = [pl.BlockSpec((tm, dg), row)] * 5 + [pl.BlockSpec((tm, 2 * dg), row), pl.BlockSpec((tm, dg), row),
                                                    pl.BlockSpec((tm, dg), row), pl.BlockSpec((tm, GATE_PAD), row)]
    if with_kmean:
        nblk = tm // MOBA_BLOCK
        out_shapes.append(jax.ShapeDtypeStruct((n // tm, nblk, dg), F32))
        out_specs.append(pl.BlockSpec((None, nblk, dg), lambda i: (i, 0, 0)))
    return pl.pallas_call(
        functools.partial(_inproj_kernel, with_kmean=with_kmean),
        grid=grid,
        in_specs=[pl.BlockSpec((tm, d), row),
                  pl.BlockSpec((1, d), lambda i: (0, 0)),
                  modspec(1), modspec(0),
                  pl.BlockSpec((d, nw), lambda i: (0, 0))],
        out_specs=out_specs,
        out_shape=out_shapes,
        compiler_params=_cparams(("arbitrary",), VMEM_LIMIT),
        name="in_proj",
    )(x, g, mod, mod, w_pad)


def _moba_kernel(q_ref, k_ref, v_ref, km_ref, o_ref, m_sc, l_sc, acc_sc, sel_sc, *, tq):
    qi = pl.program_id(2)
    nb = km_ref.shape[0]
    q32 = q_ref[...]
    lane = lax.broadcasted_iota(I32, (tq, LANES), 1)
    head_lanes = [lane < HEAD_DIM, lane >= HEAD_DIM]
    kmh, kml = _split2(km_ref[...])
    col = lax.broadcasted_iota(I32, (tq, nb), 1)
    row_i = lax.broadcasted_iota(I32, (tq, MOBA_BLOCK), 0)
    col_i = lax.broadcasted_iota(I32, (tq, MOBA_BLOCK), 1)
    causal = col_i <= row_i
    start = pl.multiple_of(qi * MOBA_BLOCK, MOBA_BLOCK)
    kd = k_ref[pl.ds(start, MOBA_BLOCK), :]
    vd = v_ref[pl.ds(start, MOBA_BLOCK), :]
    qb = []
    for h in range(2):
        qh32 = jnp.where(head_lanes[h], q32, 0.0)
        qhi, qlo = _split2(qh32)
        gate = _dot_nt(qhi, kmh) + _dot_nt(qhi, kml) + _dot_nt(qlo, kmh)
        gate = jnp.where(col < qi, gate, NEG)
        for r in range(MOBA_TOPK):
            mx = jnp.max(gate, axis=-1, keepdims=True)
            ix = jnp.min(jnp.where(gate == mx, col, nb), axis=-1, keepdims=True)
            gate = jnp.where(col == ix, -jnp.inf, gate)
            sel_sc[h * MOBA_TOPK + r] = jnp.where(r < qi, ix, -1)
        qbh = (qh32 * (HEAD_DIM ** -0.5)).astype(BF16)
        qb.append(qbh)
        s = _dot_nt(qbh, kd)
        s = jnp.where(causal, s, NEG)
        m = jnp.max(s, axis=-1, keepdims=True)
        p = jnp.exp(s - m)
        m_sc[h] = m
        l_sc[h] = jnp.sum(p, axis=-1, keepdims=True)
        acc_sc[h] = _dot(p.astype(BF16), vd)

    def body(j, carry):
        st = pl.multiple_of(j * MOBA_BLOCK, MOBA_BLOCK)
        kj = k_ref[pl.ds(st, MOBA_BLOCK), :]
        vj = v_ref[pl.ds(st, MOBA_BLOCK), :]
        for h in range(2):
            b = h * MOBA_TOPK
            rowsel = (sel_sc[b] == j) | (sel_sc[b + 1] == j) | (sel_sc[b + 2] == j)
            s = _dot_nt(qb[h], kj)
            s = jnp.where(rowsel, s, NEG)
            m_old = m_sc[h]
            m_new = jnp.maximum(m_old, jnp.max(s, axis=-1, keepdims=True))
            alpha = jnp.exp(m_old - m_new)
            p = jnp.exp(s - m_new)
            m_sc[h] = m_new
            l_sc[h] = alpha * l_sc[h] + jnp.sum(p, axis=-1, keepdims=True)
            acc_sc[h] = alpha * acc_sc[h] + _dot(p.astype(BF16), vj)
        return carry

    lax.fori_loop(0, qi, body, 0)
    o0 = acc_sc[0] / l_sc[0]
    o1 = acc_sc[1] / l_sc[1]
    o_ref[...] = jnp.where(head_lanes[0], o0, o1).astype(o_ref.dtype)


def moba_prompt(q, kb, vb, kmean, *, batch, seq):
    tq = MOBA_BLOCK
    nq = seq // tq
    nb = seq // MOBA_BLOCK
    hp = D_GROUP // LANES
    q3 = q.reshape(batch, seq, D_GROUP)
    k3 = kb.reshape(batch, seq, D_GROUP)
    v3 = vb.reshape(batch, seq, D_GROUP)
    km3 = kmean.reshape(batch, nb, D_GROUP)
    out = pl.pallas_call(
        functools.partial(_moba_kernel, tq=tq),
        grid=(batch, hp, nq),
        in_specs=[pl.BlockSpec((None, tq, LANES), lambda b, p, i: (b, i, p)),
                  pl.BlockSpec((None, seq, LANES), lambda b, p, i: (b, 0, p)),
                  pl.BlockSpec((None, seq, LANES), lambda b, p, i: (b, 0, p)),
                  pl.BlockSpec((None, nb, LANES), lambda b, p, i: (b, 0, p))],
        out_specs=pl.BlockSpec((None, tq, LANES), lambda b, p, i: (b, i, p)),
        out_shape=jax.ShapeDtypeStruct((batch, seq, D_GROUP), BF16),
        scratch_shapes=[pltpu.VMEM((2, tq, 1), F32), pltpu.VMEM((2, tq, 1), F32),
                        pltpu.VMEM((2, tq, LANES), F32), pltpu.VMEM((2 * MOBA_TOPK, tq, 1), I32)],
        compiler_params=_cparams(("arbitrary", "arbitrary", "arbitrary"), VMEM_LIMIT),
        name="moba_prompt",
    )(q3, k3, v3, km3)
    return out.reshape(batch * seq, D_GROUP)


def _log_sigmoid(x):
    return jnp.minimum(x, 0.0) - jnp.log1p(jnp.exp(-jnp.abs(x)))


def _mlstm_kernel(qk_ref, v_ref, om_ref, g_ref, cw_ref, cb_ref, bg_ref, gh_ref,
                  conv0_ref, c0_ref, n0_ref, m0_ref,
                  h_ref, cout_ref, nout_ref, mout_ref,
                  conv_sc, c_sc, n_sc, m_sc, *, L, t_valid):
    ci = pl.program_id(1)
    nh, dh, dg = N_HEADS, HEAD_DIM, D_GROUP

    @pl.when(ci == 0)
    def _():
        conv_sc[...] = conv0_ref[...]
        c_sc[...] = c0_ref[...]
        n_sc[...] = n0_ref[...]
        m_sc[...] = m0_ref[...]

    x = qk_ref[...]
    xc = jnp.concatenate([conv_sc[...], x], axis=0)
    y = cb_ref[...]
    for j in range(CONV_W):
        off = 8 - (CONV_W - 1) + j
        y = y + xc[off:off + L] * cw_ref[j:j + 1, :]
    conv_sc[...] = x[L - 8:L]
    qk = y * jax.nn.sigmoid(y)
    q_all = qk[:, :dg].astype(BF16)
    k_all = qk[:, dg:] * (dh ** -0.5)
    v_all = v_ref[...]
    om = om_ref[...]

    g = g_ref[...] + bg_ref[...]
    lf = _log_sigmoid(g)
    ri = lax.broadcasted_iota(I32, (L, L), 0)
    cj = lax.broadcasted_iota(I32, (L, L), 1)
    causal = cj <= ri
    tril = jnp.where(causal, 1.0, 0.0).astype(BF16)
    lf3 = _split3(lf)
    b_col = _dot(tril, lf3[0]) + _dot(tril, lf3[1]) + _dot(tril, lf3[2])
    sel_r = lax.broadcasted_iota(I32, (8, LANES), 0)
    sel_c = lax.broadcasted_iota(I32, (8, LANES), 1)
    sel_i = jnp.where(sel_c == sel_r, 1.0, 0.0).astype(BF16)
    sel_f = jnp.where(sel_c == sel_r + nh, 1.0, 0.0).astype(BF16)

    def rows_of(sel, a):
        a3 = _split3(a)
        return _dot_nt(sel, a3[0]) + _dot_nt(sel, a3[1]) + _dot_nt(sel, a3[2])

    li_row = rows_of(sel_i, g)
    b_row = rows_of(sel_f, b_col)
    rvalid = lax.broadcasted_iota(I32, (L, 1), 0) < t_valid
    outs = []
    for h in range(nh):
        hs = slice(h * dh, (h + 1) * dh)
        bc = b_col[:, nh + h:nh + h + 1]
        lic = g[:, h:h + 1]
        log_d = jnp.where(causal, bc - b_row[h:h + 1, :] + li_row[h:h + 1, :], -jnp.inf)
        m_prev = m_sc[h:h + 1, 0:1]
        log_inter = m_prev + bc
        m_t = jnp.maximum(log_inter, jnp.max(log_d, axis=-1, keepdims=True))
        dw = jnp.exp(log_d - m_t)
        w_inter = jnp.exp(log_inter - m_t)
        qh = q_all[:, hs]
        kh = k_all[:, hs]
        vh = v_all[:, hs]
        c_h = c_sc[h]
        n_h = n_sc[h:h + 1, :]
        s = _dot_nt(qh, kh.astype(BF16)) * dw
        num = _dot(s.astype(BF16), vh) + w_inter * _dot(qh, c_h.astype(BF16))
        qn = jnp.sum(qh.astype(F32) * n_h, axis=-1, keepdims=True)
        den = jnp.sum(s, axis=-1, keepdims=True) + w_inter * qn
        hh = num / jnp.maximum(jnp.abs(den), jnp.exp(-m_t))
        m_new = m_t[t_valid - 1:t_valid, :]
        b_last = bc[t_valid - 1:t_valid, :]
        w_state = jnp.where(rvalid, jnp.exp(b_last - bc + lic - m_new), 0.0)
        decay = jnp.exp(m_prev + b_last - m_new)
        kw = kh * w_state
        c_sc[h] = decay * c_h + lax.dot_general(kw.astype(BF16), vh, _TN, preferred_element_type=F32)
        n_sc[h:h + 1, :] = decay * n_h + jnp.sum(kw, axis=0, keepdims=True)
        m_sc[h:h + 1, :] = jnp.broadcast_to(m_new, (1, LANES))
        hn = hh * lax.rsqrt(jnp.mean(hh * hh, axis=-1, keepdims=True) + EPS) * gh_ref[:, hs]
        outs.append(hn * jax.nn.sigmoid(om[:, hs]))
    h_ref[...] = jnp.concatenate(outs, axis=-1).astype(h_ref.dtype)
    cout_ref[...] = c_sc[...]
    nout_ref[...] = n_sc[...]
    mout_ref[...] = m_sc[...]


def mlstm(qk, vm, om, gates, conv_w, conv_b, bg_pad, g_head, conv0, c0, n0, m0b, *, L, t_valid):
    b, t, _ = qk.shape
    nh, dh, dg = N_HEADS, HEAD_DIM, D_GROUP
    nc = t // L
    tok = lambda w: pl.BlockSpec((None, L, w), lambda i, c: (i, c, 0))
    full2 = lambda r, w: pl.BlockSpec((r, w), lambda i, c: (0, 0))
    return pl.pallas_call(
        functools.partial(_mlstm_kernel, L=L, t_valid=t_valid),
        grid=(b, nc),
        in_specs=[tok(2 * dg), tok(dg), tok(dg), tok(GATE_PAD),
                  full2(CONV_W, 2 * dg), full2(1, 2 * dg), full2(1, GATE_PAD), full2(1, dg),
                  pl.BlockSpec((None, 8, 2 * dg), lambda i, c: (i, 0, 0)),
                  pl.BlockSpec((None, nh, dh, dh), lambda i, c: (i, 0, 0, 0)),
                  pl.BlockSpec((None, nh, dh), lambda i, c: (i, 0, 0)),
                  pl.BlockSpec((None, nh, LANES), lambda i, c: (i, 0, 0))],
        out_specs=[tok(dg),
                   pl.BlockSpec((None, nh, dh, dh), lambda i, c: (i, 0, 0, 0)),
                   pl.BlockSpec((None, nh, dh), lambda i, c: (i, 0, 0)),
                   pl.BlockSpec((None, nh, LANES), lambda i, c: (i, 0, 0))],
        out_shape=[jax.ShapeDtypeStruct((b, t, dg), BF16),
                   jax.ShapeDtypeStruct((b, nh, dh, dh), F32),
                   jax.ShapeDtypeStruct((b, nh, dh), F32),
                   jax.ShapeDtypeStruct((b, nh, LANES), F32)],
        scratch_shapes=[pltpu.VMEM((8, 2 * dg), F32), pltpu.VMEM((nh, dh, dh), F32),
                        pltpu.VMEM((nh, dh), F32), pltpu.VMEM((nh, LANES), F32)],
        compiler_params=_cparams(("arbitrary", "arbitrary"), VMEM_LIMIT),
        name="mlstm",
    )(qk, vm, om, gates, conv_w, conv_b, bg_pad, g_head, conv0, c0, n0, m0b)


def _outproj_kernel(att_ref, hm_ref, x_ref, gt_ref, sc_ref, sh_ref, g_ref, wo_ref, wrh_ref, wrl_ref, br_ref,
                    xo_ref, h2_ref, ti_ref, tw_ref):
    dg = D_GROUP
    mix = _dot(att_ref[...], wo_ref[0:dg, :]) + _dot(hm_ref[...], wo_ref[dg:2 * dg, :])
    xn = x_ref[...] + gt_ref[...] * mix
    xo_ref[...] = xn
    ms = jnp.mean(xn * xn, axis=-1, keepdims=True)
    y = xn * lax.rsqrt(ms + EPS) * g_ref[...]
    h2 = y * (1.0 + sc_ref[...]) + sh_ref[...]
    hh, hl = _split2(h2)
    h2_ref[...] = hh
    wrh = wrh_ref[...]
    logits = _dot(hh, wrh) + _dot(hh, wrl_ref[...]) + _dot(hl, wrh) + br_ref[...]
    col = lax.broadcasted_iota(I32, logits.shape, 1)
    lg = jnp.where(col < N_EXPERTS, logits, -jnp.inf)
    vals, idxs = [], []
    for _ in range(TOP_K):
        mx = jnp.max(lg, axis=-1, keepdims=True)
        ix = jnp.min(jnp.where(lg == mx, col, LANES), axis=-1, keepdims=True)
        lg = jnp.where(col == ix, -jnp.inf, lg)
        vals.append(mx)
        idxs.append(ix)
    es = [jnp.exp(v - vals[0]) for v in vals]
    den = es[0] + es[1] + es[2] + es[3]
    ti = jnp.zeros(logits.shape, I32)
    tw = jnp.zeros(logits.shape, F32)
    for r in range(TOP_K):
        ti = jnp.where(col == r, idxs[r], ti)
        tw = jnp.where(col == r, es[r] / den, tw)
    ti_ref[...] = ti
    tw_ref[...] = tw


def out_proj(att, hm, x, mod, g_ffn, w_out, wr_hi, wr_lo, br_pad, *, tm, tiles_per_group):
    n, d = x.shape
    rm = mod.shape[1]
    dg = D_GROUP
    row = lambda i: (i, 0)
    modspec = lambda c: pl.BlockSpec((None, rm, d), lambda i, c=c: (i // tiles_per_group, 0, c))
    const = lambda r, w: pl.BlockSpec((r, w), lambda i: (0, 0))
    return pl.pallas_call(
        _outproj_kernel,
        grid=(n // tm,),
        in_specs=[pl.BlockSpec((tm, dg), row), pl.BlockSpec((tm, dg), row), pl.BlockSpec((tm, d), row),
                  modspec(2), modspec(4), modspec(3),
                  const(1, d), const(2 * dg, d), const(d, LANES), const(d, LANES), const(1, LANES)],
        out_specs=[pl.BlockSpec((tm, d), row), pl.BlockSpec((tm, d), row),
                   pl.BlockSpec((tm, LANES), row), pl.BlockSpec((tm, LANES), row)],
        out_shape=[jax.ShapeDtypeStruct((n, d), F32), jax.ShapeDtypeStruct((n, d), BF16),
                   jax.ShapeDtypeStruct((n, LANES), I32), jax.ShapeDtypeStruct((n, LANES), F32)],
        compiler_params=_cparams(("arbitrary",), VMEM_LIMIT),
        name="out_proj",
    )(att, hm, x, mod, mod, mod, g_ffn, w_out, wr_hi, wr_lo, br_pad)


def _moe_kernel(be_ref, first_ref, nused_ref, x_ref, wgu_ref, bgu_ref, wd_ref, bd_ref, y_ref, wgu_sc, wd_sc):
    i = pl.program_id(0)
    f = wd_ref.shape[0]

    @pl.when(first_ref[i] == 1)
    def _():
        wgu_sc[...] = wgu_ref[...].astype(BF16)
        wd_sc[...] = wd_ref[...].astype(BF16)

    @pl.when(i < nused_ref[0])
    def _():
        gu = _dot(x_ref[...], wgu_sc[...]) + bgu_ref[...]
        gg = jnp.minimum(gu[:, :f], SWIGLU_LIMIT)
        uu = jnp.clip(gu[:, f:], -SWIGLU_LIMIT, SWIGLU_LIMIT)
        act = (uu + 1.0) * gg * jax.nn.sigmoid(gg * SWIGLU_ALPHA)
        y_ref[...] = _dot(act.astype(BF16), wd_sc[...]) + bd_ref[...]


def moe_mlp(blk_exp, blk_first, n_used, xg, w_gu, b_gu, w_down, b_down, *, bm):
    m, d = xg.shape
    e, _, f2 = w_gu.shape
    f = f2 // 2
    nblk = m // bm
    xmap = lambda i, be, fi, nu: (jnp.minimum(i, nu[0] - 1), 0)
    return pl.pallas_call(
        _moe_kernel,
        grid_spec=pltpu.PrefetchScalarGridSpec(
            num_scalar_prefetch=3,
            grid=(nblk,),
            in_specs=[pl.BlockSpec((bm, d), xmap),
                      pl.BlockSpec((None, d, f2), lambda i, be, fi, nu: (be[i], 0, 0)),
                      pl.BlockSpec((None, 1, f2), lambda i, be, fi, nu: (be[i], 0, 0)),
                      pl.BlockSpec((None, f, d), lambda i, be, fi, nu: (be[i], 0, 0)),
                      pl.BlockSpec((None, 1, d), lambda i, be, fi, nu: (be[i], 0, 0))],
            out_specs=pl.BlockSpec((bm, d), xmap),
            scratch_shapes=[pltpu.VMEM((d, f2), BF16), pltpu.VMEM((f, d), BF16)]),
        out_shape=jax.ShapeDtypeStruct((m, d), F32),
        compiler_params=_cparams(("arbitrary",), VMEM_LIMIT),
        name="moe_mlp",
    )(blk_exp, blk_first, n_used, xg, w_gu, b_gu.reshape(e, 1, f2), w_down, b_down.reshape(e, 1, d))


def _rms_kernel(x_ref, g_ref, o_ref):
    x = x_ref[...]
    o_ref[...] = x * lax.rsqrt(jnp.mean(x * x, axis=-1, keepdims=True) + EPS) * g_ref[...]


def final_norm(x, g, *, tm):
    n, d = x.shape
    return pl.pallas_call(
        _rms_kernel,
        grid=(n // tm,),
        in_specs=[pl.BlockSpec((tm, d), lambda i: (i, 0)), pl.BlockSpec((1, d), lambda i: (0, 0))],
        out_specs=pl.BlockSpec((tm, d), lambda i: (i, 0)),
        out_shape=jax.ShapeDtypeStruct((n, d), F32),
        compiler_params=_cparams(("arbitrary",)),
        name="final_norm",
    )(x, g)


def _moba_sample_glue(q, k, v, q_pos0):
    B, H, T, d = q.shape
    Lk = k.shape[2]
    NB = max(-(-Lk // MOBA_BLOCK), MOBA_TOPK)
    pad = NB * MOBA_BLOCK - Lk
    k = jnp.pad(k, ((0, 0), (0, 0), (0, pad), (0, 0)))
    v = jnp.pad(v, ((0, 0), (0, 0), (0, pad), (0, 0)))
    kb = k.reshape(B, H, NB, MOBA_BLOCK, d)
    vb = v.reshape(B, H, NB, MOBA_BLOCK, d)
    k_mean = kb.mean(axis=3)
    pos = q_pos0 + jnp.arange(T, dtype=I32)
    own = pos // MOBA_BLOCK
    gate = jnp.einsum('bhqd,bhnd->bhqn', q, k_mean, precision=lax.Precision.HIGHEST)
    blk_ids = jnp.arange(NB, dtype=I32)
    gate = jnp.where(blk_ids[None, :] < own[:, None], gate, NEG)
    _, sel = lax.top_k(gate, MOBA_TOPK)
    blocks = jnp.concatenate([sel.astype(I32), jnp.broadcast_to(own[None, None, :, None], (B, H, T, 1))], axis=-1)
    rank = jnp.arange(MOBA_TOPK, dtype=I32)
    blk_ok = jnp.concatenate([rank[None, :] < own[:, None], jnp.ones((T, 1), bool)], axis=-1)
    bi = jnp.arange(B)[:, None, None, None]
    hi = jnp.arange(H)[None, :, None, None]
    kg = kb[bi, hi, blocks]
    vg = vb[bi, hi, blocks]
    offs = jnp.arange(MOBA_BLOCK, dtype=I32)
    key_pos = blocks[..., None] * MOBA_BLOCK + offs
    mask = blk_ok[None, None, :, :, None] & (key_pos <= pos[None, None, :, None, None])
    s = jnp.einsum('bhqd,bhqjkd->bhqjk', q, kg) * (d ** -0.5)
    s = jnp.where(mask, s, NEG)
    p = jax.nn.softmax(s.reshape(B, H, T, -1), axis=-1).reshape(s.shape)
    return jnp.einsum('bhqjk,bhqjkd->bhqd', p, vg)


def _route(ti, tw, bm):
    n = ti.shape[0]
    na = n * TOP_K
    e_flat = ti[:, :TOP_K].reshape(-1)
    order = jnp.argsort(e_flat, stable=True).astype(I32)
    e_sorted = e_flat[order]
    tok_sorted = order // TOP_K
    counts = jnp.zeros((N_EXPERTS,), I32).at[e_flat].add(1)
    padded = ((counts + bm - 1) // bm) * bm
    pad_end = jnp.cumsum(padded)
    pad_start = pad_end - padded
    grp_start = jnp.cumsum(counts) - counts
    dest = pad_start[e_sorted] + jnp.arange(na, dtype=I32) - grp_start[e_sorted]
    nblk = -(-(na + N_EXPERTS * (bm - 1)) // bm)
    m = nblk * bm
    row_tok = jnp.zeros((m,), I32).at[dest].set(tok_sorted)
    blk_start = jnp.arange(nblk, dtype=I32) * bm
    blk_exp = jnp.minimum(jnp.searchsorted(pad_end, blk_start, side='right'), N_EXPERTS - 1).astype(I32)
    n_used = (pad_end[-1] // bm).astype(I32)
    blk_first = jnp.concatenate([jnp.ones((1,), I32), (blk_exp[1:] != blk_exp[:-1]).astype(I32)])
    blk_first = jnp.where(jnp.arange(nblk) < n_used, blk_first, 0).astype(I32)
    pos = jnp.zeros((na,), I32).at[order].set(dest).reshape(n, TOP_K)
    return row_tok, blk_exp, blk_first, n_used.reshape(1), pos


def _layer(l, xp, xs, modp, mods, P, S, W):
    dg = D_GROUP
    Bp, T = P["batch"], P["seq"]
    Bs, Ts = S["batch"], S["seq"]
    d = xp.shape[1]
    tm = 512
    qa, ka, va, kb, vb, qk, vm, om, gt, kmean = in_proj(
        xp, modp, W["g_mix"][l], W["w_in"][l], tm=tm, tiles_per_group=T // tm, with_kmean=True)
    att_p = moba_prompt(qa, kb, vb, kmean.reshape(-1, dg), batch=Bp, seq=T)
    Lp = 256
    hm_p, c_p, n_p, m_p = mlstm(
        qk.reshape(Bp, T, 2 * dg), vm.reshape(Bp, T, dg), om.reshape(Bp, T, dg), gt.reshape(Bp, T, GATE_PAD),
        W["conv_w"][l], W["conv_b"][l], W["bg_pad"][l], W["g_head"][l],
        jnp.zeros((Bp, 8, 2 * dg), F32), jnp.zeros((Bp, N_HEADS, HEAD_DIM, HEAD_DIM), F32),
        jnp.zeros((Bp, N_HEADS, HEAD_DIM), F32), jnp.zeros((Bp, N_HEADS, LANES), F32), L=Lp, t_valid=Lp)
    conv_p = qk.reshape(Bp, T, 2 * dg)[:, T - (CONV_W - 1):]
    xp1, h2p, tip, twp = out_proj(att_p, hm_p.reshape(Bp * T, dg), xp, modp, W["g_ffn"][l], W["w_out"][l],
                                  W["wr_hi"][l], W["wr_lo"][l], W["br_pad"][l], tm=tm, tiles_per_group=T // tm)
    ns = Bs * Ts
    qa_s, ka_s, va_s, _, _, qk_s, vm_s, om_s, gt_s = in_proj(
        xs, mods, W["g_mix"][l], W["w_in"][l], tm=ns, tiles_per_group=1, with_kmean=False)
    past = S["page_table"].shape[1] * S["cache_k"].shape[2]
    k_past = S["cache_k"][l][S["page_table"]].reshape(Bs, past, N_HEADS, HEAD_DIM)
    v_past = S["cache_v"][l][S["page_table"]].reshape(Bs, past, N_HEADS, HEAD_DIM)
    heads = lambda a: a.reshape(Bs, -1, N_HEADS, HEAD_DIM).transpose(0, 2, 1, 3)
    k_all = jnp.concatenate([k_past, ka_s.reshape(Bs, Ts, N_HEADS, HEAD_DIM)], axis=1)
    v_all = jnp.concatenate([v_past, va_s.reshape(Bs, Ts, N_HEADS, HEAD_DIM)], axis=1)
    att_s = _moba_sample_glue(heads(qa_s), heads(k_all), heads(v_all), past)
    att_s = att_s.transpose(0, 2, 1, 3).reshape(ns, dg).astype(BF16)
    Ls = 128
    padt = lambda a: jnp.pad(a.reshape(Bs, Ts, -1), ((0, 0), (0, Ls - Ts), (0, 0)))
    conv0 = jnp.pad(S["state_conv"][l], ((0, 0), (8 - (CONV_W - 1), 0), (0, 0)))
    m0b = jnp.broadcast_to(S["state_m"][l][:, :, None], (Bs, N_HEADS, LANES))
    hm_s, c_s, n_s, m_s = mlstm(padt(qk_s), padt(vm_s), padt(om_s), padt(gt_s),
                                W["conv_w"][l], W["conv_b"][l], W["bg_pad"][l], W["g_head"][l],
                                conv0, S["state_c"][l], S["state_n"][l], m0b, L=Ls, t_valid=Ts)
    hm_s = hm_s[:, :Ts].reshape(ns, dg)
    conv_s = qk_s.reshape(Bs, Ts, 2 * dg)[:, Ts - (CONV_W - 1):]
    xs1, h2s, tis, tws = out_proj(att_s, hm_s, xs, mods, W["g_ffn"][l], W["w_out"][l],
                                  W["wr_hi"][l], W["wr_lo"][l], W["br_pad"][l], tm=ns, tiles_per_group=1)
    h2 = jnp.concatenate([h2p, h2s], axis=0)
    ti = jnp.concatenate([tip, tis], axis=0)
    tw = jnp.concatenate([twp, tws], axis=0)
    row_tok, blk_exp, blk_first, n_used, pos = _route(ti, tw, MOE_BM)
    xg = h2[row_tok]
    yb = moe_mlp(blk_exp, blk_first, n_used, xg, W["w_gu"][l], W["b_gu"][l], W["w_down"][l], W["b_down"][l], bm=MOE_BM)
    ff = jnp.sum(yb[pos] * tw[:, :TOP_K, None], axis=1)
    np_ = Bp * T
    gt2p = jnp.repeat(modp[:, 0, 5 * d:6 * d], T, axis=0)
    xp2 = xp1 + gt2p * ff[:np_]
    xs2 = xs1 + mods[0, :, 5 * d:6 * d] * ff[np_:]
    outs_p = (ka.reshape(Bp, T, N_HEADS, HEAD_DIM), va.reshape(Bp, T, N_HEADS, HEAD_DIM), c_p, n_p, m_p[:, :, 0], conv_p)
    outs_s = (ka_s.reshape(Bs, Ts, N_HEADS, HEAD_DIM), va_s.reshape(Bs, Ts, N_HEADS, HEAD_DIM), c_s, n_s, m_s[:, :, 0], conv_s)
    return xp2, xs2, outs_p, outs_s


def kernel(x_prompt, x_sample, c_prompt, c_sample, cache_k, cache_v, page_table, state_c, state_n, state_m,
           state_conv, w_ada, b_ada, g_mix, w_in, b_gate, conv_w, conv_b, g_head, w_out, g_ffn, w_router,
           b_router, w_gu, b_gu, w_down, b_down, g_final):
    depth = w_ada.shape[0]
    Bp, T, d = x_prompt.shape
    Bs, Ts, _ = x_sample.shape
    dg = D_GROUP
    n_gate = 2 * N_HEADS
    w_in_pad = jnp.pad(w_in, ((0, 0), (0, 0), (0, GATE_PAD - n_gate))).astype(BF16)
    wr_pad = jnp.pad(w_router, ((0, 0), (0, 0), (0, LANES - N_EXPERTS)))
    wr_hi = wr_pad.astype(BF16)
    wr_lo = (wr_pad - wr_hi.astype(F32)).astype(BF16)
    W = dict(
        g_mix=g_mix.reshape(depth, 1, d), w_in=w_in_pad,
        conv_w=conv_w, conv_b=conv_b.reshape(depth, 1, 2 * dg),
        bg_pad=jnp.pad(b_gate, ((0, 0), (0, GATE_PAD - n_gate))).reshape(depth, 1, GATE_PAD),
        g_head=g_head.reshape(depth, 1, dg), w_out=w_out.astype(BF16), g_ffn=g_ffn.reshape(depth, 1, d),
        wr_hi=wr_hi, wr_lo=wr_lo,
        br_pad=jnp.pad(b_router, ((0, 0), (0, LANES - N_EXPERTS))).reshape(depth, 1, LANES),
        w_gu=w_gu, b_gu=b_gu, w_down=w_down, b_down=b_down)
    n_pool = cache_k.shape[1]
    S = dict(batch=Bs, seq=Ts, page_table=page_table,
             cache_k=cache_k.reshape(depth, n_pool, cache_k.shape[2], dg),
             cache_v=cache_v.reshape(depth, n_pool, cache_v.shape[2], dg),
             state_c=state_c, state_n=state_n, state_m=state_m, state_conv=state_conv)
    P = dict(batch=Bp, seq=T)
    nc = Bp + Bs
    rpad = -(-nc // 8) * 8
    c_all = jnp.pad(jnp.concatenate([c_prompt, c_sample], axis=0), ((0, rpad - nc), (0, 0)))
    mod = ada_mod(c_all, w_ada, b_ada)
    xp = x_prompt.reshape(Bp * T, d)
    xs = x_sample.reshape(Bs * Ts, d)
    acc_p, acc_s = [], []
    for l in range(depth):
        modp = mod[l, :Bp].reshape(Bp, 1, N_ADA * d)
        mods = jnp.repeat(mod[l, Bp:nc], Ts, axis=0).reshape(1, Bs * Ts, N_ADA * d)
        xp, xs, op, os_ = _layer(l, xp, xs, modp, mods, P, S, W)
        acc_p.append(op)
        acc_s.append(os_)
    y_p = final_norm(xp, g_final.reshape(1, d), tm=512).reshape(Bp, T, d)
    y_s = final_norm(xs, g_final.reshape(1, d), tm=Bs * Ts).reshape(Bs, Ts, d)
    stack = lambda acc, i: jnp.stack([a[i] for a in acc])
    return (y_p, y_s) + tuple(stack(acc_p, i) for i in range(6)) + tuple(stack(acc_s, i) for i in range(6))
```

```python
import functools

import jax
import jax.numpy as jnp
from jax import lax
from jax.experimental import pallas as pl
from jax.experimental.pallas import tpu as pltpu

F32 = jnp.float32
BF16 = jnp.bfloat16
I32 = jnp.int32

HEAD_DIM = 64
N_HEADS = 8
D_GROUP = N_HEADS * HEAD_DIM
MOBA_BLOCK = 256
MOBA_TOPK = 3
CONV_W = 4
N_EXPERTS = 32
TOP_K = 4
SWIGLU_ALPHA = 1.702
SWIGLU_LIMIT = 7.0
N_ADA = 6
EPS = 1e-6
NEG = -1e30
LANES = 128
GATE_PAD = LANES
MOE_BM = 512
VMEM_LIMIT = 56 * 1024 * 1024


def _cparams(sem, vmem=None):
    return pltpu.CompilerParams(dimension_semantics=sem, vmem_limit_bytes=vmem)


def _split2(x):
    hi = x.astype(BF16)
    lo = (x - hi.astype(F32)).astype(BF16)
    return hi, lo


def _split3(x):
    hi = x.astype(BF16)
    r = x - hi.astype(F32)
    mid = r.astype(BF16)
    lo = (r - mid.astype(F32)).astype(BF16)
    return hi, mid, lo


_NT = (((1,), (1,)), ((), ()))
_TN = (((0,), (0,)), ((), ()))


def _dot(a, b):
    return jnp.dot(a, b, preferred_element_type=F32)


def _dot_nt(a, b):
    return lax.dot_general(a, b, _NT, preferred_element_type=F32)


def _ada_kernel(c_ref, w_ref, b_ref, o_ref):
    c = c_ref[...]
    s = (c * jax.nn.sigmoid(c)).astype(BF16)
    o_ref[...] = _dot(s, w_ref[...].astype(BF16)) + b_ref[...]


def ada_mod(c_all, w_ada, b_ada):
    depth, d, nd = w_ada.shape
    r = c_all.shape[0]
    nj = nd // d
    return pl.pallas_call(
        _ada_kernel,
        grid=(depth, nj),
        in_specs=[pl.BlockSpec((r, d), lambda l, j: (0, 0)),
                  pl.BlockSpec((None, d, d), lambda l, j: (l, 0, j)),
                  pl.BlockSpec((None, 1, d), lambda l, j: (l, 0, j))],
        out_specs=pl.BlockSpec((None, r, d), lambda l, j: (l, 0, j)),
        out_shape=jax.ShapeDtypeStruct((depth, r, nd), F32),
        compiler_params=_cparams(("arbitrary", "arbitrary")),
        name="ada_mod",
    )(c_all, w_ada, b_ada.reshape(depth, 1, nd))


def _inproj_kernel(x_ref, g_ref, sc_ref, sh_ref, w_ref, *rest, fresh, aliased):
    if aliased:
        rest = rest[2:]
    qa_ref, ks_ref, vs_ref, kb_ref, vb_ref, qk_ref, vm_ref, om_ref, gt_ref = rest[:9]
    x = x_ref[...]
    ms = jnp.mean(x * x, axis=-1, keepdims=True)
    y = x * lax.rsqrt(ms + EPS) * g_ref[...]
    h = (y * (1.0 + sc_ref[...]) + sh_ref[...]).astype(BF16)
    dg = D_GROUP
    tm = x.shape[0]

    def seg(a, b):
        return _dot(h, w_ref[:, a:b])

    qa_ref[...] = seg(0, dg)
    ka = seg(dg, 2 * dg)
    va = seg(2 * dg, 3 * dg)
    kb_ref[...] = ka.astype(BF16)
    if fresh:
        ks_ref[...] = ka.T
        vt = va.T
        vs_ref[...] = vt
        vb_ref[...] = vt.astype(BF16)
        km_ref = rest[9]
        nblk = tm // MOBA_BLOCK
        rows = [jnp.sum(ka[i * MOBA_BLOCK:(i + 1) * MOBA_BLOCK], axis=0, keepdims=True) for i in range(nblk)]
        km_ref[...] = jnp.concatenate(rows, axis=0) * (1.0 / MOBA_BLOCK)
    else:
        ks_ref[...] = ka
        vs_ref[...] = va
        vb_ref[...] = va.astype(BF16)
    qk_ref[...] = seg(3 * dg, 5 * dg)
    vm_ref[...] = seg(5 * dg, 6 * dg).astype(BF16)
    om_ref[...] = seg(6 * dg, 7 * dg)
    gt_ref[...] = seg(7 * dg, 7 * dg + GATE_PAD)


def in_proj(x, mod, g, w_pad, kstack, vstack, *, layer, depth, tm, tiles_per_group, fresh):
    n, d = x.shape
    rm = mod.shape[1]
    nw = w_pad.shape[1]
    dg = D_GROUP
    grid = (n // tm,)
    row = lambda i: (i, 0)
    modspec = lambda c: pl.BlockSpec((None, rm, d), lambda i, c=c: (i // tiles_per_group, 0, c))
    if fresh:
        ngrp = n // (tm * tiles_per_group)
        seq = tm * tiles_per_group
        tmap = lambda i: (i // tiles_per_group, 0, i % tiles_per_group)
        kv_shape = jax.ShapeDtypeStruct((depth, ngrp, dg, seq), F32)
        kv_spec = pl.BlockSpec((None, None, dg, tm), lambda i: (layer,) + tmap(i))
        vb_shape = jax.ShapeDtypeStruct((ngrp, dg, seq), BF16)
        vb_spec = pl.BlockSpec((None, dg, tm), tmap)
    else:
        kv_shape = jax.ShapeDtypeStruct((n, dg), F32)
        kv_spec = pl.BlockSpec((tm, dg), row)
        vb_shape = jax.ShapeDtypeStruct((n, dg), BF16)
        vb_spec = pl.BlockSpec((tm, dg), row)
    out_shapes = [jax.ShapeDtypeStruct((n, dg), F32), kv_shape, kv_shape,
                  jax.ShapeDtypeStruct((n, dg), BF16), vb_shape,
                  jax.ShapeDtypeStruct((n, 2 * dg), F32), jax.ShapeDtypeStruct((n, dg), BF16),
                  jax.ShapeDtypeStruct((n, dg), F32), jax.ShapeDtypeStruct((n, GATE_PAD), F32)]
    out_specs = [pl.BlockSpec((tm, dg), row), kv_spec, kv_spec,
                 pl.BlockSpec((tm, dg), row), vb_spec,
                 pl.BlockSpec((tm, 2 * dg), row), pl.BlockSpec((tm, dg), row),
                 pl.BlockSpec((tm, dg), row), pl.BlockSpec((tm, GATE_PAD), row)]
    if fresh:
        nblk = tm // MOBA_BLOCK
        out_shapes.append(jax.ShapeDtypeStruct((n // tm, nblk, dg), F32))
        out_specs.append(pl.BlockSpec((None, nblk, dg), lambda i: (i, 0, 0)))
    in_specs = [pl.BlockSpec((tm, d), row),
                pl.BlockSpec((1, d), lambda i: (0, 0)),
                modspec(1), modspec(0),
                pl.BlockSpec((d, nw), lambda i: (0, 0))]
    args = [x, g, mod, mod, w_pad]
    aliased = kstack is not None
    aliases = {}
    if aliased:
        in_specs += [pl.BlockSpec(memory_space=pl.ANY)] * 2
        args += [kstack, vstack]
        aliases = {5: 1, 6: 2}
    return pl.pallas_call(
        functools.partial(_inproj_kernel, fresh=fresh, aliased=aliased),
        grid=grid,
        in_specs=in_specs,
        out_specs=out_specs,
        out_shape=out_shapes,
        input_output_aliases=aliases,
        compiler_params=_cparams(("arbitrary",), VMEM_LIMIT),
        name="in_proj",
    )(*args)


L_ROWS = 16
MOBA_G = 4


def _moba_kernel(q_ref, k_ref, v_ref, km_ref, o_ref, vt_sc, qb_sc, m_sc, acc_sc, sel_sc, *, tq, seq):
    qi = pl.program_id(2)
    nb = km_ref.shape[0]
    blk = MOBA_BLOCK
    nd = LANES + L_ROWS
    G = MOBA_G
    kb_sc = k_ref

    @pl.when(qi == 0)
    def _():
        vt_sc[0:LANES, :] = v_ref[...]
        vt_sc[LANES:nd, :] = jnp.ones((L_ROWS, seq), BF16)

    qt = q_ref[...].T
    drow = lax.broadcasted_iota(I32, (LANES, tq), 0)
    head_rows = [drow < HEAD_DIM, drow >= HEAD_DIM]
    kmh, kml = _split2(km_ref[...])
    bidx = lax.broadcasted_iota(I32, (nb, tq), 0)
    key_i = lax.broadcasted_iota(I32, (blk, tq), 0)
    qry_i = lax.broadcasted_iota(I32, (blk, tq), 1)
    causal = key_i <= qry_i
    start = pl.multiple_of(qi * blk, blk)
    kd = kb_sc[pl.ds(start, blk), :]
    vd = vt_sc[:, pl.ds(start, blk)]
    for h in range(2):
        qh = jnp.where(head_rows[h], qt, 0.0)
        qhi, qlo = _split2(qh)
        gate = _dot(kmh, qhi) + _dot(kmh, qlo) + _dot(kml, qhi)
        gate = jnp.where(bidx < qi, gate, NEG)
        for r in range(MOBA_TOPK):
            mx = jnp.max(gate, axis=0, keepdims=True)
            ix = jnp.min(jnp.where(gate == mx, bidx, nb), axis=0, keepdims=True)
            gate = jnp.where(bidx == ix, -jnp.inf, gate)
            sel_sc[h * MOBA_TOPK + r] = jnp.where(r < qi, ix, -1)
        qb = (qh * (HEAD_DIM ** -0.5)).astype(BF16)
        qb_sc[h] = qb
        s = _dot(kd, qb)
        s = jnp.where(causal, s, NEG)
        m = jnp.max(s, axis=0, keepdims=True)
        p = jnp.exp(s - m)
        m_sc[h] = m
        acc_sc[h] = _dot(vd, p.astype(BF16))

    def body(g, carry):
        kjs, vjs, js = [], [], []
        for u in range(G):
            j = g * G + u
            st = pl.multiple_of(jnp.minimum(j, nb - 1) * blk, blk)
            kjs.append(kb_sc[pl.ds(st, blk), :])
            vjs.append(vt_sc[:, pl.ds(st, blk)])
            js.append(j)
        ss = [[_dot(kjs[u], qb_sc[h]) for u in range(G)] for h in range(2)]
        ps, alphas = [], []
        for h in range(2):
            b = h * MOBA_TOPK
            sm = []
            for u in range(G):
                rowsel = (sel_sc[b] == js[u]) | (sel_sc[b + 1] == js[u]) | (sel_sc[b + 2] == js[u])
                sm.append(jnp.where(rowsel, ss[h][u], NEG))
            m_old = m_sc[h]
            m_new = m_old
            for u in range(G):
                m_new = jnp.maximum(m_new, jnp.max(sm[u], axis=0, keepdims=True))
            alphas.append(jnp.exp(m_old - m_new))
            ps.append(jnp.concatenate([jnp.exp(sm[u] - m_new).astype(BF16) for u in range(G)], axis=0))
            m_sc[h] = m_new
        vcat = jnp.concatenate(vjs, axis=1)
        pv = [_dot(vcat, ps[h]) for h in range(2)]
        for h in range(2):
            acc_sc[h] = alphas[h] * acc_sc[h] + pv[h]
        return carry

    lax.fori_loop(0, (qi + (G - 1)) // G, body, 0)
    a0 = acc_sc[0]
    a1 = acc_sc[1]
    o0 = a0[0:LANES] / a0[LANES:LANES + 1]
    o1 = a1[0:LANES] / a1[LANES:LANES + 1]
    o_ref[...] = jnp.where(head_rows[0], o0, o1).T.astype(o_ref.dtype)


def moba_prompt(q, k, vt, kmean, *, batch, seq):
    tq = MOBA_BLOCK
    nq = seq // tq
    nb = seq // MOBA_BLOCK
    hp = D_GROUP // LANES
    q3 = q.reshape(batch, seq, D_GROUP)
    k3 = k.reshape(batch, seq, D_GROUP)
    km3 = kmean.reshape(batch, nb, D_GROUP)
    nd = LANES + L_ROWS
    out = pl.pallas_call(
        functools.partial(_moba_kernel, tq=tq, seq=seq),
        grid=(batch, hp, nq),
        in_specs=[pl.BlockSpec((None, tq, LANES), lambda b, p, i: (b, i, p)),
                  pl.BlockSpec((None, seq, LANES), lambda b, p, i: (b, 0, p)),
                  pl.BlockSpec((None, LANES, seq), lambda b, p, i: (b, p, 0)),
                  pl.BlockSpec((None, nb, LANES), lambda b, p, i: (b, 0, p))],
        out_specs=pl.BlockSpec((None, tq, LANES), lambda b, p, i: (b, i, p)),
        out_shape=jax.ShapeDtypeStruct((batch, seq, D_GROUP), BF16),
        scratch_shapes=[pltpu.VMEM((nd, seq), BF16),
                        pltpu.VMEM((2, LANES, tq), BF16), pltpu.VMEM((2, 1, tq), F32),
                        pltpu.VMEM((2, nd, tq), F32), pltpu.VMEM((2 * MOBA_TOPK, 1, tq), I32)],
        compiler_params=_cparams(("arbitrary", "arbitrary", "arbitrary"), VMEM_LIMIT),
        name="moba_prompt",
    )(q3, k3, vt, km3)
    return out.reshape(batch * seq, D_GROUP)


DEC_PG = 8
DEC_NEW = 16
DEC_COLS = 32


def _moba_decode_kernel(pt_ref, q2_ref, kn_ref, vn_ref, *rest, n_steps, page, n_q):
    k_refs = rest[:DEC_PG]
    v_refs = rest[DEC_PG:2 * DEC_PG]
    o_ref = rest[2 * DEC_PG]
    s_sc, gate_sc, acc_sc, l_sc, pown_sc = rest[2 * DEC_PG + 1:]
    s = pl.program_id(1)
    ppb = MOBA_BLOCK // page
    nblk = n_steps * DEC_PG // ppb
    nc = DEC_COLS
    scale = HEAD_DIM ** -0.5
    q2 = q2_ref[...]
    lane = lax.broadcasted_iota(I32, (nc, LANES), 1)

    @pl.when(s == 0)
    def _():
        gate_sc[...] = jnp.zeros(gate_sc.shape, F32)

    @pl.when(s < n_steps)
    def _():
        g = gate_sc[...]
        for i in range(DEC_PG):
            pidx = s * DEC_PG + i
            sp = _dot(q2, k_refs[i][...].astype(BF16))
            sf = sp[0:nc] + sp[nc:2 * nc]
            s_sc[:, pl.ds(pl.multiple_of(pidx * page, page), page)] = sf
            g = g + jnp.where(lane == pidx // ppb, jnp.sum(sf, axis=1, keepdims=True), 0.0)
        gate_sc[...] = g

    @pl.when(s == n_steps - 1)
    def _():
        gate = jnp.where(lane < nblk, gate_sc[...] * (1.0 / MOBA_BLOCK), -jnp.inf)
        sels = []
        for r in range(MOBA_TOPK):
            mx = jnp.max(gate, axis=1, keepdims=True)
            ix = jnp.min(jnp.where(gate == mx, lane, LANES), axis=1, keepdims=True)
            gate = jnp.where(lane == ix, -jnp.inf, gate)
            sels.append(ix)
        so = _dot_nt(q2, kn_ref[...].astype(BF16))
        so = (so[0:nc] + so[nc:2 * nc]) * scale
        key_c = lax.broadcasted_iota(I32, so.shape, 1)
        q_r = lax.broadcasted_iota(I32, so.shape, 0) // N_HEADS
        own_ok = key_c <= q_r
        so = jnp.where(own_ok, so, NEG)
        npos = s_sc.shape[1]
        blk_of = lax.broadcasted_iota(I32, (nc, npos), 1) // MOBA_BLOCK
        mask = (blk_of == sels[0]) | (blk_of == sels[1]) | (blk_of == sels[2])
        sc = jnp.where(mask, s_sc[...] * scale, NEG)
        m = jnp.maximum(jnp.max(sc, axis=1, keepdims=True), jnp.max(so, axis=1, keepdims=True))
        p = jnp.where(mask, jnp.exp(sc - m), 0.0)
        po = jnp.where(own_ok, jnp.exp(so - m), 0.0)
        s_sc[...] = p
        pown_sc[...] = po
        l_sc[...] = jnp.sum(p, axis=1, keepdims=True) + jnp.sum(po, axis=1, keepdims=True)
        acc_sc[...] = jnp.zeros(acc_sc.shape, F32)

    @pl.when(s >= n_steps)
    def _():
        acc = acc_sc[...]
        for i in range(DEC_PG):
            pidx = (s - n_steps) * DEC_PG + i
            pj = s_sc[:, pl.ds(pl.multiple_of(pidx * page, page), page)].astype(BF16)
            acc = acc + _dot_nt(pj, v_refs[i][...].astype(BF16))
        acc_sc[...] = acc

    @pl.when(s == 2 * n_steps - 1)
    def _():
        acc = acc_sc[...] + _dot(pown_sc[...].astype(BF16), vn_ref[...].astype(BF16))
        out = acc / l_sc[...]
        hrow = lax.broadcasted_iota(I32, (N_HEADS, D_GROUP), 0)
        hlane = lax.broadcasted_iota(I32, (N_HEADS, D_GROUP), 1) // HEAD_DIM
        rows = []
        for q in range(n_q):
            blk8 = out[q * N_HEADS:(q + 1) * N_HEADS, :]
            rows.append(jnp.sum(jnp.where(hrow == hlane, blk8, 0.0), axis=0, keepdims=True))
        rows.append(jnp.zeros((8 - n_q, D_GROUP), F32))
        o_ref[...] = jnp.concatenate(rows, axis=0)


def moba_decode(page_table, q2, k_new, v_new, cache_kt, cache_vt, *, layer, n_q):
    bsz, n_pages = page_table.shape
    page = cache_kt.shape[3]
    assert n_pages % DEC_PG == 0 and MOBA_BLOCK % page == 0 and page % LANES == 0
    assert n_q <= 8 and n_q * N_HEADS <= DEC_COLS and MOBA_TOPK <= n_pages * page // MOBA_BLOCK <= LANES
    n_steps = n_pages // DEC_PG
    kspec = lambda i: pl.BlockSpec(
        (None, None, D_GROUP, page),
        lambda b, s, pt, i=i: (layer, pt[b, jnp.minimum(s, n_steps - 1) * DEC_PG + i], 0, 0))
    vspec = lambda i: pl.BlockSpec(
        (None, None, D_GROUP, page),
        lambda b, s, pt, i=i: (layer, pt[b, jnp.maximum(s - n_steps, 0) * DEC_PG + i], 0, 0))
    perb = lambda r, w: pl.BlockSpec((None, r, w), lambda b, s, pt: (b, 0, 0))
    return pl.pallas_call(
        functools.partial(_moba_decode_kernel, n_steps=n_steps, page=page, n_q=n_q),
        grid_spec=pltpu.PrefetchScalarGridSpec(
            num_scalar_prefetch=1,
            grid=(bsz, 2 * n_steps),
            in_specs=[perb(2 * DEC_COLS, D_GROUP), perb(DEC_NEW, D_GROUP), perb(DEC_NEW, D_GROUP)]
            + [kspec(i) for i in range(DEC_PG)] + [vspec(i) for i in range(DEC_PG)],
            out_specs=perb(8, D_GROUP),
            scratch_shapes=[pltpu.VMEM((DEC_COLS, n_pages * page), F32), pltpu.VMEM((DEC_COLS, LANES), F32),
                            pltpu.VMEM((DEC_COLS, D_GROUP), F32), pltpu.VMEM((DEC_COLS, 1), F32),
                            pltpu.VMEM((DEC_COLS, DEC_NEW), F32)]),
        out_shape=jax.ShapeDtypeStruct((bsz, 8, D_GROUP), F32),
        compiler_params=_cparams(("arbitrary", "arbitrary"), VMEM_LIMIT),
        name="moba_decode",
    )(page_table, q2, k_new, v_new, *([cache_kt] * DEC_PG), *([cache_vt] * DEC_PG))


def _log_sigmoid(x):
    return jnp.minimum(x, 0.0) - jnp.log1p(jnp.exp(-jnp.abs(x)))


def _mlstm_kernel(qk_ref, v_ref, om_ref, g_ref, cw_ref, cb_ref, bg_ref, gh_ref,
                  conv0_ref, c0_ref, n0_ref, m0_ref,
                  h_ref, cout_ref, nout_ref, mout_ref,
                  conv_sc, c_sc, n_sc, m_sc, *, L, t_valid):
    ci = pl.program_id(1)
    nh, dh, dg = N_HEADS, HEAD_DIM, D_GROUP

    @pl.when(ci == 0)
    def _():
        conv_sc[...] = conv0_ref[...]
        c_sc[...] = c0_ref[...]
        n_sc[...] = n0_ref[...]
        m_sc[...] = m0_ref[...]

    x = qk_ref[...]
    xc = jnp.concatenate([conv_sc[...], x], axis=0)
    y = cb_ref[...]
    for j in range(CONV_W):
        off = 8 - (CONV_W - 1) + j
        y = y + xc[off:off + L] * cw_ref[j:j + 1, :]
    conv_sc[...] = x[L - 8:L]
    qk = y * jax.nn.sigmoid(y)
    q_all = qk[:, :dg].astype(BF16)
    k_all = qk[:, dg:] * (dh ** -0.5)
    v_all = v_ref[...]
    om = om_ref[...]

    g = g_ref[...] + bg_ref[...]
    lf = _log_sigmoid(g)
    ri = lax.broadcasted_iota(I32, (L, L), 0)
    cj = lax.broadcasted_iota(I32, (L, L), 1)
    causal = cj <= ri
    tril = jnp.where(causal, 1.0, 0.0).astype(BF16)
    lf3 = _split3(lf)
    b_col = _dot(tril, lf3[0]) + _dot(tril, lf3[1]) + _dot(tril, lf3[2])
    sel_r = lax.broadcasted_iota(I32, (8, LANES), 0)
    sel_c = lax.broadcasted_iota(I32, (8, LANES), 1)
    sel_i = jnp.where(sel_c == sel_r, 1.0, 0.0).astype(BF16)
    sel_f = jnp.where(sel_c == sel_r + nh, 1.0, 0.0).astype(BF16)

    def rows_of(sel, a):
        a3 = _split3(a)
        return _dot_nt(sel, a3[0]) + _dot_nt(sel, a3[1]) + _dot_nt(sel, a3[2])

    li_row = rows_of(sel_i, g)
    b_row = rows_of(sel_f, b_col)
    rvalid = lax.broadcasted_iota(I32, (L, 1), 0) < t_valid
    outs = []
    for h in range(nh):
        hs = slice(h * dh, (h + 1) * dh)
        bc = b_col[:, nh + h:nh + h + 1]
        lic = g[:, h:h + 1]
        log_d = jnp.where(causal, bc - b_row[h:h + 1, :] + li_row[h:h + 1, :], -jnp.inf)
        m_prev = m_sc[h:h + 1, 0:1]
        log_inter = m_prev + bc
        m_t = jnp.maximum(log_inter, jnp.max(log_d, axis=-1, keepdims=True))
        dw = jnp.exp(log_d - m_t)
        w_inter = jnp.exp(log_inter - m_t)
        qh = q_all[:, hs]
        kh = k_all[:, hs]
        vh = v_all[:, hs]
        c_h = c_sc[h]
        n_h = n_sc[h:h + 1, :]
        s = _dot_nt(qh, kh.astype(BF16)) * dw
        num = _dot(s.astype(BF16), vh) + w_inter * _dot(qh, c_h.astype(BF16))
        qn = jnp.sum(qh.astype(F32) * n_h, axis=-1, keepdims=True)
        den = jnp.sum(s, axis=-1, keepdims=True) + w_inter * qn
        hh = num / jnp.maximum(jnp.abs(den), jnp.exp(-m_t))
        m_new = m_t[t_valid - 1:t_valid, :]
        b_last = bc[t_valid - 1:t_valid, :]
        w_state = jnp.where(rvalid, jnp.exp(b_last - bc + lic - m_new), 0.0)
        decay = jnp.exp(m_prev + b_last - m_new)
        kw = kh * w_state
        c_sc[h] = decay * c_h + lax.dot_general(kw.astype(BF16), vh, _TN, preferred_element_type=F32)
        n_sc[h:h + 1, :] = decay * n_h + jnp.sum(kw, axis=0, keepdims=True)
        m_sc[h:h + 1, :] = jnp.broadcast_to(m_new, (1, LANES))
        hn = hh * lax.rsqrt(jnp.mean(hh * hh, axis=-1, keepdims=True) + EPS) * gh_ref[:, hs]
        outs.append(hn * jax.nn.sigmoid(om[:, hs]))
    h_ref[...] = jnp.concatenate(outs, axis=-1).astype(h_ref.dtype)
    cout_ref[...] = c_sc[...]
    nout_ref[...] = n_sc[...]
    mout_ref[...] = m_sc[...]


def mlstm(qk, vm, om, gates, conv_w, conv_b, bg_pad, g_head, conv0, c0, n0, m0b, *, L, t_valid):
    b, t, _ = qk.shape
    nh, dh, dg = N_HEADS, HEAD_DIM, D_GROUP
    nc = t // L
    tok = lambda w: pl.BlockSpec((None, L, w), lambda i, c: (i, c, 0))
    full2 = lambda r, w: pl.BlockSpec((r, w), lambda i, c: (0, 0))
    return pl.pallas_call(
        functools.partial(_mlstm_kernel, L=L, t_valid=t_valid),
        grid=(b, nc),
        in_specs=[tok(2 * dg), tok(dg), tok(dg), tok(GATE_PAD),
                  full2(CONV_W, 2 * dg), full2(1, 2 * dg), full2(1, GATE_PAD), full2(1, dg),
                  pl.BlockSpec((None, 8, 2 * dg), lambda i, c: (i, 0, 0)),
                  pl.BlockSpec((None, nh, dh, dh), lambda i, c: (i, 0, 0, 0)),
                  pl.BlockSpec((None, nh, dh), lambda i, c: (i, 0, 0)),
                  pl.BlockSpec((None, nh, LANES), lambda i, c: (i, 0, 0))],
        out_specs=[tok(dg),
                   pl.BlockSpec((None, nh, dh, dh), lambda i, c: (i, 0, 0, 0)),
                   pl.BlockSpec((None, nh, dh), lambda i, c: (i, 0, 0)),
                   pl.BlockSpec((None, nh, LANES), lambda i, c: (i, 0, 0))],
        out_shape=[jax.ShapeDtypeStruct((b, t, dg), BF16),
                   jax.ShapeDtypeStruct((b, nh, dh, dh), F32),
                   jax.ShapeDtypeStruct((b, nh, dh), F32),
                   jax.ShapeDtypeStruct((b, nh, LANES), F32)],
        scratch_shapes=[pltpu.VMEM((8, 2 * dg), F32), pltpu.VMEM((nh, dh, dh), F32),
                        pltpu.VMEM((nh, dh), F32), pltpu.VMEM((nh, LANES), F32)],
        compiler_params=_cparams(("arbitrary", "arbitrary"), VMEM_LIMIT),
        name="mlstm",
    )(qk, vm, om, gates, conv_w, conv_b, bg_pad, g_head, conv0, c0, n0, m0b)


def _outproj_kernel(att_ref, hm_ref, x_ref, gt_ref, sc_ref, sh_ref, g_ref, wo_ref, wrh_ref, wrl_ref, br_ref,
                    xo_ref, h2_ref, ti_ref, tw_ref):
    dg = D_GROUP
    mix = _dot(att_ref[...], wo_ref[0:dg, :]) + _dot(hm_ref[...], wo_ref[dg:2 * dg, :])
    xn = x_ref[...] + gt_ref[...] * mix
    xo_ref[...] = xn
    ms = jnp.mean(xn * xn, axis=-1, keepdims=True)
    y = xn * lax.rsqrt(ms + EPS) * g_ref[...]
    h2 = y * (1.0 + sc_ref[...]) + sh_ref[...]
    hh, hl = _split2(h2)
    h2_ref[...] = hh
    wrh = wrh_ref[...]
    logits = _dot(hh, wrh) + _dot(hh, wrl_ref[...]) + _dot(hl, wrh) + br_ref[...]
    col = lax.broadcasted_iota(I32, logits.shape, 1)
    lg = jnp.where(col < N_EXPERTS, logits, -jnp.inf)
    vals, idxs = [], []
    for _ in range(TOP_K):
        mx = jnp.max(lg, axis=-1, keepdims=True)
        ix = jnp.min(jnp.where(lg == mx, col, LANES), axis=-1, keepdims=True)
        lg = jnp.where(col == ix, -jnp.inf, lg)
        vals.append(mx)
        idxs.append(ix)
    es = [jnp.exp(v - vals[0]) for v in vals]
    den = es[0] + es[1] + es[2] + es[3]
    ti = jnp.zeros(logits.shape, I32)
    tw = jnp.zeros(logits.shape, F32)
    for r in range(TOP_K):
        ti = jnp.where(col == r, idxs[r], ti)
        tw = jnp.where(col == r, es[r] / den, tw)
    ti_ref[...] = ti
    tw_ref[...] = tw


def out_proj(att, hm, x, mod, g_ffn, w_out, wr_hi, wr_lo, br_pad, *, tm, tiles_per_group):
    n, d = x.shape
    rm = mod.shape[1]
    dg = D_GROUP
    row = lambda i: (i, 0)
    modspec = lambda c: pl.BlockSpec((None, rm, d), lambda i, c=c: (i // tiles_per_group, 0, c))
    const = lambda r, w: pl.BlockSpec((r, w), lambda i: (0, 0))
    return pl.pallas_call(
        _outproj_kernel,
        grid=(n // tm,),
        in_specs=[pl.BlockSpec((tm, dg), row), pl.BlockSpec((tm, dg), row), pl.BlockSpec((tm, d), row),
                  modspec(2), modspec(4), modspec(3),
                  const(1, d), const(2 * dg, d), const(d, LANES), const(d, LANES), const(1, LANES)],
        out_specs=[pl.BlockSpec((tm, d), row), pl.BlockSpec((tm, d), row),
                   pl.BlockSpec((tm, LANES), row), pl.BlockSpec((tm, LANES), row)],
        out_shape=[jax.ShapeDtypeStruct((n, d), F32), jax.ShapeDtypeStruct((n, d), BF16),
                   jax.ShapeDtypeStruct((n, LANES), I32), jax.ShapeDtypeStruct((n, LANES), F32)],
        compiler_params=_cparams(("arbitrary",), VMEM_LIMIT),
        name="out_proj",
    )(att, hm, x, mod, mod, mod, g_ffn, w_out, wr_hi, wr_lo, br_pad)


def _rank_kernel(ti_ref, rank_ref, cnt_ref, base_sc):
    i = pl.program_id(0)

    @pl.when(i == 0)
    def _():
        base_sc[...] = jnp.zeros(base_sc.shape, F32)

    ti = ti_ref[...]
    tm = ti.shape[0]
    col = lax.broadcasted_iota(I32, ti.shape, 1)
    es = []
    oh = jnp.zeros(ti.shape, F32)
    for k in range(TOP_K):
        ek = jnp.sum(jnp.where(col == k, ti, 0), axis=-1, keepdims=True)
        es.append(ek)
        oh = oh + jnp.where(col == ek, 1.0, 0.0)
    ri = lax.broadcasted_iota(I32, (tm, tm), 0)
    ci = lax.broadcasted_iota(I32, (tm, tm), 1)
    tri = jnp.where(ci < ri, 1.0, 0.0).astype(BF16)
    rank = _dot(tri, oh.astype(BF16)) + base_sc[0:1, :]
    out = jnp.zeros(ti.shape, F32)
    for k in range(TOP_K):
        rk = jnp.sum(jnp.where(col == es[k], rank, 0.0), axis=-1, keepdims=True)
        out = jnp.where(col == k, rk, out)
    rank_ref[...] = out.astype(I32)
    tot = base_sc[0:1, :] + jnp.sum(oh, axis=0, keepdims=True)
    base_sc[...] = jnp.broadcast_to(tot, base_sc.shape)
    cnt_ref[...] = base_sc[...].astype(I32)


def expert_ranks(ti, *, tm):
    n = ti.shape[0]
    return pl.pallas_call(
        _rank_kernel,
        grid=(n // tm,),
        in_specs=[pl.BlockSpec((tm, LANES), lambda i: (i, 0))],
        out_specs=[pl.BlockSpec((tm, LANES), lambda i: (i, 0)), pl.BlockSpec((8, LANES), lambda i: (0, 0))],
        out_shape=[jax.ShapeDtypeStruct((n, LANES), I32), jax.ShapeDtypeStruct((8, LANES), I32)],
        scratch_shapes=[pltpu.VMEM((8, LANES), F32)],
        compiler_params=_cparams(("arbitrary",)),
        name="expert_ranks",
    )(ti)


def _moe_kernel(be_ref, first_ref, nused_ref, x_ref, wgu_ref, bgu_ref, wd_ref, bd_ref, y_ref, wgu_sc, wd_sc):
    i = pl.program_id(0)
    f = wd_ref.shape[0]

    @pl.when(first_ref[i] == 1)
    def _():
        wgu_sc[...] = wgu_ref[...].astype(BF16)
        wd_sc[...] = wd_ref[...].astype(BF16)

    @pl.when(i < nused_ref[0])
    def _():
        gu = _dot(x_ref[...], wgu_sc[...]) + bgu_ref[...]
        gg = jnp.minimum(gu[:, :f], SWIGLU_LIMIT)
        uu = jnp.clip(gu[:, f:], -SWIGLU_LIMIT, SWIGLU_LIMIT)
        act = (uu + 1.0) * gg * jax.nn.sigmoid(gg * SWIGLU_ALPHA)
        y_ref[...] = _dot(act.astype(BF16), wd_sc[...]) + bd_ref[...]


def moe_mlp(blk_exp, blk_first, n_used, xg, w_gu, b_gu, w_down, b_down, *, bm):
    m, d = xg.shape
    e, _, f2 = w_gu.shape
    f = f2 // 2
    nblk = m // bm
    xmap = lambda i, be, fi, nu: (jnp.minimum(i, nu[0] - 1), 0)
    return pl.pallas_call(
        _moe_kernel,
        grid_spec=pltpu.PrefetchScalarGridSpec(
            num_scalar_prefetch=3,
            grid=(nblk,),
            in_specs=[pl.BlockSpec((bm, d), xmap),
                      pl.BlockSpec((None, d, f2), lambda i, be, fi, nu: (be[i], 0, 0)),
                      pl.BlockSpec((None, 1, f2), lambda i, be, fi, nu: (be[i], 0, 0)),
                      pl.BlockSpec((None, f, d), lambda i, be, fi, nu: (be[i], 0, 0)),
                      pl.BlockSpec((None, 1, d), lambda i, be, fi, nu: (be[i], 0, 0))],
            out_specs=pl.BlockSpec((bm, d), xmap),
            scratch_shapes=[pltpu.VMEM((d, f2), BF16), pltpu.VMEM((f, d), BF16)]),
        out_shape=jax.ShapeDtypeStruct((m, d), F32),
        compiler_params=_cparams(("arbitrary",), VMEM_LIMIT),
        name="moe_mlp",
    )(blk_exp, blk_first, n_used, xg, w_gu, b_gu.reshape(e, 1, f2), w_down, b_down.reshape(e, 1, d))


def _rms_kernel(x_ref, g_ref, o_ref):
    x = x_ref[...]
    o_ref[...] = x * lax.rsqrt(jnp.mean(x * x, axis=-1, keepdims=True) + EPS) * g_ref[...]


def final_norm(x, g, *, tm):
    n, d = x.shape
    return pl.pallas_call(
        _rms_kernel,
        grid=(n // tm,),
        in_specs=[pl.BlockSpec((tm, d), lambda i: (i, 0)), pl.BlockSpec((1, d), lambda i: (0, 0))],
        out_specs=pl.BlockSpec((tm, d), lambda i: (i, 0)),
        out_shape=jax.ShapeDtypeStruct((n, d), F32),
        compiler_params=_cparams(("arbitrary",)),
        name="final_norm",
    )(x, g)


def _decode_q2(q):
    b, ts, _ = q.shape
    q4 = q.reshape(b, ts, N_HEADS, HEAD_DIM)
    eye = jnp.eye(N_HEADS, dtype=F32)
    qbd = q4[:, :, None, :, :] * eye[None, None, :, :, None]
    qbd = qbd.reshape(b, ts * N_HEADS, D_GROUP)
    qbd = jnp.pad(qbd, ((0, 0), (0, DEC_COLS - ts * N_HEADS), (0, 0)))
    hi = qbd.astype(BF16)
    lo = (qbd - hi.astype(F32)).astype(BF16)
    return jnp.concatenate([hi, lo], axis=1)


def _route(ti, bm):
    n = ti.shape[0]
    na = n * TOP_K
    tm = max(t for t in range(8, 513, 8) if n % t == 0)
    rank, cnt = expert_ranks(ti, tm=tm)
    counts = cnt[0, :N_EXPERTS]
    padded = ((counts + bm - 1) // bm) * bm
    pad_end = jnp.cumsum(padded)
    pad_start = pad_end - padded
    e4 = ti[:, :TOP_K]
    onehot = e4[:, :, None] == jnp.arange(N_EXPERTS, dtype=I32)[None, None, :]
    pos = jnp.sum(jnp.where(onehot, pad_start[None, None, :], 0), axis=-1) + rank[:, :TOP_K]
    nblk = -(-(na + N_EXPERTS * (bm - 1)) // bm)
    m = nblk * bm
    tok = jnp.broadcast_to(jnp.arange(n, dtype=I32)[:, None], (n, TOP_K))
    row_tok = jnp.zeros((m,), I32).at[pos.reshape(-1)].set(tok.reshape(-1), unique_indices=True)
    blk_start = jnp.arange(nblk, dtype=I32) * bm
    blk_exp = jnp.minimum(jnp.sum(blk_start[:, None] >= pad_end[None, :], axis=1), N_EXPERTS - 1).astype(I32)
    n_used = (pad_end[-1] // bm).astype(I32)
    blk_first = jnp.concatenate([jnp.ones((1,), I32), (blk_exp[1:] != blk_exp[:-1]).astype(I32)])
    blk_first = jnp.where(jnp.arange(nblk) < n_used, blk_first, 0).astype(I32)
    return row_tok, blk_exp, blk_first, n_used.reshape(1), pos


def _layer(l, depth, xp, xs, modp, mods, P, S, W, stacks):
    dg = D_GROUP
    Bp, T = P["batch"], P["seq"]
    Bs, Ts = S["batch"], S["seq"]
    d = xp.shape[1]
    tm = 512
    kst_p, vst_p = stacks
    qa, kst_p, vst_p, kb, vbt, qk, vm, om, gt, kmean = in_proj(
        xp, modp, W["g_mix"][l], W["w_in"][l], kst_p, vst_p, layer=l, depth=depth,
        tm=tm, tiles_per_group=T // tm, fresh=True)
    att_p = moba_prompt(qa, kb, vbt, kmean.reshape(-1, dg), batch=Bp, seq=T)
    Lp = 256
    hm_p, c_p, n_p, m_p = mlstm(
        qk.reshape(Bp, T, 2 * dg), vm.reshape(Bp, T, dg), om.reshape(Bp, T, dg), gt.reshape(Bp, T, GATE_PAD),
        W["conv_w"][l], W["conv_b"][l], W["bg_pad"][l], W["g_head"][l],
        jnp.zeros((Bp, 8, 2 * dg), F32), jnp.zeros((Bp, N_HEADS, HEAD_DIM, HEAD_DIM), F32),
        jnp.zeros((Bp, N_HEADS, HEAD_DIM), F32), jnp.zeros((Bp, N_HEADS, LANES), F32), L=Lp, t_valid=Lp)
    conv_p = qk.reshape(Bp, T, 2 * dg)[:, T - (CONV_W - 1):]
    xp1, h2p, tip, twp = out_proj(att_p, hm_p.reshape(Bp * T, dg), xp, modp, W["g_ffn"][l], W["w_out"][l],
                                  W["wr_hi"][l], W["wr_lo"][l], W["br_pad"][l], tm=tm, tiles_per_group=T // tm)
    ns = Bs * Ts
    qa_s, ka_s, va_s, _, _, qk_s, vm_s, om_s, gt_s = in_proj(
        xs, mods, W["g_mix"][l], W["w_in"][l], None, None, layer=l, depth=depth,
        tm=ns, tiles_per_group=1, fresh=False)
    newpad = lambda a: jnp.pad(a.reshape(Bs, Ts, dg), ((0, 0), (0, DEC_NEW - Ts), (0, 0)))
    att_s = moba_decode(S["page_table"], _decode_q2(qa_s.reshape(Bs, Ts, dg)), newpad(ka_s), newpad(va_s),
                        S["cache_kt"], S["cache_vt"], layer=l, n_q=Ts)
    att_s = att_s[:, :Ts].reshape(ns, dg).astype(BF16)
    Ls = 128
    padt = lambda a: jnp.pad(a.reshape(Bs, Ts, -1), ((0, 0), (0, Ls - Ts), (0, 0)))
    conv0 = jnp.pad(S["state_conv"][l], ((0, 0), (8 - (CONV_W - 1), 0), (0, 0)))
    m0b = jnp.broadcast_to(S["state_m"][l][:, :, None], (Bs, N_HEADS, LANES))
    hm_s, c_s, n_s, m_s = mlstm(padt(qk_s), padt(vm_s), padt(om_s), padt(gt_s),
                                W["conv_w"][l], W["conv_b"][l], W["bg_pad"][l], W["g_head"][l],
                                conv0, S["state_c"][l], S["state_n"][l], m0b, L=Ls, t_valid=Ts)
    hm_s = hm_s[:, :Ts].reshape(ns, dg)
    conv_s = qk_s.reshape(Bs, Ts, 2 * dg)[:, Ts - (CONV_W - 1):]
    xs1, h2s, tis, tws = out_proj(att_s, hm_s, xs, mods, W["g_ffn"][l], W["w_out"][l],
                                  W["wr_hi"][l], W["wr_lo"][l], W["br_pad"][l], tm=ns, tiles_per_group=1)
    h2 = jnp.concatenate([h2p, h2s], axis=0)
    ti = jnp.concatenate([tip, tis], axis=0)
    tw = jnp.concatenate([twp, tws], axis=0)
    row_tok, blk_exp, blk_first, n_used, pos = _route(ti, MOE_BM)
    xg = h2[row_tok]
    yb = moe_mlp(blk_exp, blk_first, n_used, xg, W["w_gu"][l], W["b_gu"][l], W["w_down"][l], W["b_down"][l], bm=MOE_BM)
    ff = jnp.sum(yb[pos] * tw[:, :TOP_K, None], axis=1)
    np_ = Bp * T
    gt2p = jnp.repeat(modp[:, 0, 5 * d:6 * d], T, axis=0)
    xp2 = xp1 + gt2p * ff[:np_]
    xs2 = xs1 + mods[0, :, 5 * d:6 * d] * ff[np_:]
    outs_p = (c_p, n_p, m_p[:, :, 0], conv_p)
    outs_s = (ka_s.reshape(Bs, Ts, N_HEADS, HEAD_DIM), va_s.reshape(Bs, Ts, N_HEADS, HEAD_DIM),
              c_s, n_s, m_s[:, :, 0], conv_s)
    return xp2, xs2, outs_p, outs_s, (kst_p, vst_p)


def kernel(x_prompt, x_sample, c_prompt, c_sample, cache_k, cache_v, page_table, state_c, state_n, state_m,
           state_conv, w_ada, b_ada, g_mix, w_in, b_gate, conv_w, conv_b, g_head, w_out, g_ffn, w_router,
           b_router, w_gu, b_gu, w_down, b_down, g_final):
    depth = w_ada.shape[0]
    Bp, T, d = x_prompt.shape
    Bs, Ts, _ = x_sample.shape
    dg = D_GROUP
    n_gate = 2 * N_HEADS
    w_in_pad = jnp.pad(w_in, ((0, 0), (0, 0), (0, GATE_PAD - n_gate))).astype(BF16)
    wr_pad = jnp.pad(w_router, ((0, 0), (0, 0), (0, LANES - N_EXPERTS)))
    wr_hi = wr_pad.astype(BF16)
    wr_lo = (wr_pad - wr_hi.astype(F32)).astype(BF16)
    W = dict(
        g_mix=g_mix.reshape(depth, 1, d), w_in=w_in_pad,
        conv_w=conv_w, conv_b=conv_b.reshape(depth, 1, 2 * dg),
        bg_pad=jnp.pad(b_gate, ((0, 0), (0, GATE_PAD - n_gate))).reshape(depth, 1, GATE_PAD),
        g_head=g_head.reshape(depth, 1, dg), w_out=w_out.astype(BF16), g_ffn=g_ffn.reshape(depth, 1, d),
        wr_hi=wr_hi, wr_lo=wr_lo,
        br_pad=jnp.pad(b_router, ((0, 0), (0, LANES - N_EXPERTS))).reshape(depth, 1, LANES),
        w_gu=w_gu, b_gu=b_gu, w_down=w_down, b_down=b_down)
    n_pool, page = cache_k.shape[1], cache_k.shape[2]
    assert (page_table.shape[1] * page) % MOBA_BLOCK == 0
    pages_t = lambda c: c.transpose(0, 1, 3, 4, 2).reshape(depth, n_pool, dg, page)
    S = dict(batch=Bs, seq=Ts, page_table=page_table, cache_kt=pages_t(cache_k), cache_vt=pages_t(cache_v),
             state_c=state_c, state_n=state_n, state_m=state_m, state_conv=state_conv)
    P = dict(batch=Bp, seq=T)
    nc = Bp + Bs
    rpad = -(-nc // 8) * 8
    c_all = jnp.pad(jnp.concatenate([c_prompt, c_sample], axis=0), ((0, rpad - nc), (0, 0)))
    mod = ada_mod(c_all, w_ada, b_ada)
    xp = x_prompt.reshape(Bp * T, d)
    xs = x_sample.reshape(Bs * Ts, d)
    acc_p, acc_s = [], []
    stacks = (None, None)
    for l in range(depth):
        modp = mod[l, :Bp].reshape(Bp, 1, N_ADA * d)
        mods = jnp.repeat(mod[l, Bp:nc], Ts, axis=0).reshape(1, Bs * Ts, N_ADA * d)
        xp, xs, op, os_, stacks = _layer(l, depth, xp, xs, modp, mods, P, S, W, stacks)
        acc_p.append(op)
        acc_s.append(os_)
    y_p = final_norm(xp, g_final.reshape(1, d), tm=512).reshape(Bp, T, d)
    y_s = final_norm(xs, g_final.reshape(1, d), tm=Bs * Ts).reshape(Bs, Ts, d)
    stack = lambda acc, i: jnp.stack([a[i] for a in acc])
    rows = lambda a: a.reshape(depth, Bp, N_HEADS, HEAD_DIM, T).transpose(0, 1, 4, 2, 3)
    return ((y_p, y_s, rows(stacks[0]), rows(stacks[1])) + tuple(stack(acc_p, i) for i in range(4))
            + tuple(stack(acc_s, i) for i in range(6)))
```

```python
import functools

import jax
import jax.numpy as jnp
from jax import lax
from jax.experimental import pallas as pl
from jax.experimental.pallas import tpu as pltpu

F32 = jnp.float32
BF16 = jnp.bfloat16
I32 = jnp.int32
U32 = jnp.uint32

HEAD_DIM = 64
N_HEADS = 8
D_GROUP = N_HEADS * HEAD_DIM
MOBA_BLOCK = 256
MOBA_TOPK = 3
CONV_W = 4
N_EXPERTS = 32
TOP_K = 4
SWIGLU_ALPHA = 1.702
SWIGLU_LIMIT = 7.0
N_ADA = 6
EPS = 1e-6
NEG = -1e30
LANES = 128
GATE_PAD = LANES
MOE_BM = 512
VMEM_LIMIT = 56 * 1024 * 1024


def _cparams(sem, vmem=None):
    return pltpu.CompilerParams(dimension_semantics=sem, vmem_limit_bytes=vmem)


def _split2(x):
    hi = x.astype(BF16)
    lo = (x - hi.astype(F32)).astype(BF16)
    return hi, lo


def _split3(x):
    hi = x.astype(BF16)
    r = x - hi.astype(F32)
    mid = r.astype(BF16)
    lo = (r - mid.astype(F32)).astype(BF16)
    return hi, mid, lo


_NT = (((1,), (1,)), ((), ()))
_TN = (((0,), (0,)), ((), ()))


def _dot(a, b):
    return jnp.dot(a, b, preferred_element_type=F32)


def _dot_nt(a, b):
    return lax.dot_general(a, b, _NT, preferred_element_type=F32)


def _ada_kernel(c_ref, w_ref, b_ref, o_ref):
    c = c_ref[...]
    s = (c * jax.nn.sigmoid(c)).astype(BF16)
    o_ref[...] = _dot(s, w_ref[...].astype(BF16)) + b_ref[...]


def ada_mod(c_all, w_ada, b_ada):
    depth, d, nd = w_ada.shape
    r = c_all.shape[0]
    nj = nd // d
    return pl.pallas_call(
        _ada_kernel,
        grid=(depth, nj),
        in_specs=[pl.BlockSpec((r, d), lambda l, j: (0, 0)),
                  pl.BlockSpec((None, d, d), lambda l, j: (l, 0, j)),
                  pl.BlockSpec((None, 1, d), lambda l, j: (l, 0, j))],
        out_specs=pl.BlockSpec((None, r, d), lambda l, j: (l, 0, j)),
        out_shape=jax.ShapeDtypeStruct((depth, r, nd), F32),
        compiler_params=_cparams(("arbitrary", "arbitrary")),
        name="ada_mod",
    )(c_all, w_ada, b_ada.reshape(depth, 1, nd))


def _inproj_kernel(x_ref, g_ref, sc_ref, sh_ref, w_ref, *rest, fresh, aliased):
    if aliased:
        rest = rest[2:]
    qa_ref, ks_ref, vs_ref, kb_ref, vb_ref, qk_ref, vm_ref, om_ref, gt_ref = rest[:9]
    x = x_ref[...]
    ms = jnp.mean(x * x, axis=-1, keepdims=True)
    y = x * lax.rsqrt(ms + EPS) * g_ref[...]
    h = (y * (1.0 + sc_ref[...]) + sh_ref[...]).astype(BF16)
    dg = D_GROUP
    tm = x.shape[0]

    def seg(a, b):
        return _dot(h, w_ref[:, a:b])

    qa_ref[...] = seg(0, dg)
    ka = seg(dg, 2 * dg)
    va = seg(2 * dg, 3 * dg)
    kb_ref[...] = ka.astype(BF16)
    if fresh:
        ks_ref[...] = ka.T
        vt = va.T
        vs_ref[...] = vt
        vb_ref[...] = vt.astype(BF16)
        km_ref = rest[9]
        nblk = tm // MOBA_BLOCK
        rows = [jnp.sum(ka[i * MOBA_BLOCK:(i + 1) * MOBA_BLOCK], axis=0, keepdims=True) for i in range(nblk)]
        km_ref[...] = jnp.concatenate(rows, axis=0) * (1.0 / MOBA_BLOCK)
    else:
        ks_ref[...] = ka
        vs_ref[...] = va
        vb_ref[...] = va.astype(BF16)
    qk_ref[...] = seg(3 * dg, 5 * dg)
    vm_ref[...] = seg(5 * dg, 6 * dg).astype(BF16)
    om_ref[...] = seg(6 * dg, 7 * dg)
    gt_ref[...] = seg(7 * dg, 7 * dg + GATE_PAD)


def in_proj(x, mod, g, w_pad, kstack, vstack, *, layer, depth, tm, tiles_per_group, fresh):
    n, d = x.shape
    rm = mod.shape[1]
    nw = w_pad.shape[1]
    dg = D_GROUP
    grid = (n // tm,)
    row = lambda i: (i, 0)
    modspec = lambda c: pl.BlockSpec((None, rm, d), lambda i, c=c: (i // tiles_per_group, 0, c))
    if fresh:
        ngrp = n // (tm * tiles_per_group)
        seq = tm * tiles_per_group
        tmap = lambda i: (i // tiles_per_group, 0, i % tiles_per_group)
        kv_shape = jax.ShapeDtypeStruct((depth, ngrp, dg, seq), F32)
        kv_spec = pl.BlockSpec((None, None, dg, tm), lambda i: (layer,) + tmap(i))
        vb_shape = jax.ShapeDtypeStruct((ngrp, dg, seq), BF16)
        vb_spec = pl.BlockSpec((None, dg, tm), tmap)
    else:
        kv_shape = jax.ShapeDtypeStruct((n, dg), F32)
        kv_spec = pl.BlockSpec((tm, dg), row)
        vb_shape = jax.ShapeDtypeStruct((n, dg), BF16)
        vb_spec = pl.BlockSpec((tm, dg), row)
    out_shapes = [jax.ShapeDtypeStruct((n, dg), F32), kv_shape, kv_shape,
                  jax.ShapeDtypeStruct((n, dg), BF16), vb_shape,
                  jax.ShapeDtypeStruct((n, 2 * dg), F32), jax.ShapeDtypeStruct((n, dg), BF16),
                  jax.ShapeDtypeStruct((n, dg), F32), jax.ShapeDtypeStruct((n, GATE_PAD), F32)]
    out_specs = [pl.BlockSpec((tm, dg), row), kv_spec, kv_spec,
                 pl.BlockSpec((tm, dg), row), vb_spec,
                 pl.BlockSpec((tm, 2 * dg), row), pl.BlockSpec((tm, dg), row),
                 pl.BlockSpec((tm, dg), row), pl.BlockSpec((tm, GATE_PAD), row)]
    if fresh:
        nblk = tm // MOBA_BLOCK
        out_shapes.append(jax.ShapeDtypeStruct((n // tm, nblk, dg), F32))
        out_specs.append(pl.BlockSpec((None, nblk, dg), lambda i: (i, 0, 0)))
    in_specs = [pl.BlockSpec((tm, d), row),
                pl.BlockSpec((1, d), lambda i: (0, 0)),
                modspec(1), modspec(0),
                pl.BlockSpec((d, nw), lambda i: (0, 0))]
    args = [x, g, mod, mod, w_pad]
    aliased = kstack is not None
    aliases = {}
    if aliased:
        in_specs += [pl.BlockSpec(memory_space=pl.ANY)] * 2
        args += [kstack, vstack]
        aliases = {5: 1, 6: 2}
    return pl.pallas_call(
        functools.partial(_inproj_kernel, fresh=fresh, aliased=aliased),
        grid=grid,
        in_specs=in_specs,
        out_specs=out_specs,
        out_shape=out_shapes,
        input_output_aliases=aliases,
        compiler_params=_cparams(("arbitrary",), VMEM_LIMIT),
        name="in_proj",
    )(*args)


L_ROWS = 16
MOBA_G = 4


def _moba_kernel(q_ref, k_ref, v_ref, km_ref, o_ref, vt_sc, qb_sc, m_sc, acc_sc, sel_sc, *, tq, seq):
    qi = pl.program_id(2)
    nb = km_ref.shape[0]
    blk = MOBA_BLOCK
    nd = LANES + L_ROWS
    G = MOBA_G
    kb_sc = k_ref

    @pl.when(qi == 0)
    def _():
        vt_sc[0:LANES, :] = v_ref[...]
        vt_sc[LANES:nd, :] = jnp.ones((L_ROWS, seq), BF16)

    qt = q_ref[...].T
    drow = lax.broadcasted_iota(I32, (LANES, tq), 0)
    head_rows = [drow < HEAD_DIM, drow >= HEAD_DIM]
    kmh, kml = _split2(km_ref[...])
    bidx = lax.broadcasted_iota(I32, (nb, tq), 0)
    key_i = lax.broadcasted_iota(I32, (blk, tq), 0)
    qry_i = lax.broadcasted_iota(I32, (blk, tq), 1)
    causal = key_i <= qry_i
    start = pl.multiple_of(qi * blk, blk)
    kd = kb_sc[pl.ds(start, blk), :]
    vd = vt_sc[:, pl.ds(start, blk)]
    for h in range(2):
        qh = jnp.where(head_rows[h], qt, 0.0)
        qhi, qlo = _split2(qh)
        gate = _dot(kmh, qhi) + _dot(kmh, qlo) + _dot(kml, qhi)
        gate = jnp.where(bidx < qi, gate, NEG)
        for r in range(MOBA_TOPK):
            mx = jnp.max(gate, axis=0, keepdims=True)
            ix = jnp.min(jnp.where(gate == mx, bidx, nb), axis=0, keepdims=True)
            gate = jnp.where(bidx == ix, -jnp.inf, gate)
            sel_sc[h * MOBA_TOPK + r] = jnp.where(r < qi, ix, -1)
        qb = (qh * (HEAD_DIM ** -0.5)).astype(BF16)
        qb_sc[h] = qb
        s = _dot(kd, qb)
        s = jnp.where(causal, s, NEG)
        m = jnp.max(s, axis=0, keepdims=True)
        p = jnp.exp(s - m)
        m_sc[h] = m
        acc_sc[h] = _dot(vd, p.astype(BF16))

    def body(g, carry):
        kjs, vjs, js = [], [], []
        for u in range(G):
            j = g * G + u
            st = pl.multiple_of(jnp.minimum(j, nb - 1) * blk, blk)
            kjs.append(kb_sc[pl.ds(st, blk), :])
            vjs.append(vt_sc[:, pl.ds(st, blk)])
            js.append(j)
        ss = [[_dot(kjs[u], qb_sc[h]) for u in range(G)] for h in range(2)]
        ps, alphas = [], []
        for h in range(2):
            b = h * MOBA_TOPK
            sm = []
            for u in range(G):
                rowsel = (sel_sc[b] == js[u]) | (sel_sc[b + 1] == js[u]) | (sel_sc[b + 2] == js[u])
                sm.append(jnp.where(rowsel, ss[h][u], NEG))
            m_old = m_sc[h]
            m_new = m_old
            for u in range(G):
                m_new = jnp.maximum(m_new, jnp.max(sm[u], axis=0, keepdims=True))
            alphas.append(jnp.exp(m_old - m_new))
            ps.append(jnp.concatenate([jnp.exp(sm[u] - m_new).astype(BF16) for u in range(G)], axis=0))
            m_sc[h] = m_new
        vcat = jnp.concatenate(vjs, axis=1)
        pv = [_dot(vcat, ps[h]) for h in range(2)]
        for h in range(2):
            acc_sc[h] = alphas[h] * acc_sc[h] + pv[h]
        return carry

    lax.fori_loop(0, (qi + (G - 1)) // G, body, 0)
    a0 = acc_sc[0]
    a1 = acc_sc[1]
    o0 = a0[0:LANES] / a0[LANES:LANES + 1]
    o1 = a1[0:LANES] / a1[LANES:LANES + 1]
    o_ref[...] = jnp.where(head_rows[0], o0, o1).T.astype(o_ref.dtype)


def moba_prompt(q, k, vt, kmean, *, batch, seq):
    tq = MOBA_BLOCK
    nq = seq // tq
    nb = seq // MOBA_BLOCK
    hp = D_GROUP // LANES
    q3 = q.reshape(batch, seq, D_GROUP)
    k3 = k.reshape(batch, seq, D_GROUP)
    km3 = kmean.reshape(batch, nb, D_GROUP)
    nd = LANES + L_ROWS
    out = pl.pallas_call(
        functools.partial(_moba_kernel, tq=tq, seq=seq),
        grid=(batch, hp, nq),
        in_specs=[pl.BlockSpec((None, tq, LANES), lambda b, p, i: (b, i, p)),
                  pl.BlockSpec((None, seq, LANES), lambda b, p, i: (b, 0, p)),
                  pl.BlockSpec((None, LANES, seq), lambda b, p, i: (b, p, 0)),
                  pl.BlockSpec((None, nb, LANES), lambda b, p, i: (b, 0, p))],
        out_specs=pl.BlockSpec((None, tq, LANES), lambda b, p, i: (b, i, p)),
        out_shape=jax.ShapeDtypeStruct((batch, seq, D_GROUP), BF16),
        scratch_shapes=[pltpu.VMEM((nd, seq), BF16),
                        pltpu.VMEM((2, LANES, tq), BF16), pltpu.VMEM((2, 1, tq), F32),
                        pltpu.VMEM((2, nd, tq), F32), pltpu.VMEM((2 * MOBA_TOPK, 1, tq), I32)],
        compiler_params=_cparams(("arbitrary", "arbitrary", "arbitrary"), VMEM_LIMIT),
        name="moba_prompt",
    )(q3, k3, vt, km3)
    return out.reshape(batch * seq, D_GROUP)


DEC_PG = 8
DEC_NEW = 16
DEC_COLS = 32


def _moba_decode_kernel(pt_ref, q2_ref, kn_ref, vn_ref, *rest, n_steps, page, n_q):
    k_refs = rest[:DEC_PG]
    v_refs = rest[DEC_PG:2 * DEC_PG]
    o_ref = rest[2 * DEC_PG]
    s_sc, gate_sc, acc_sc, l_sc, pown_sc = rest[2 * DEC_PG + 1:]
    s = pl.program_id(1)
    ppb = MOBA_BLOCK // page
    nblk = n_steps * DEC_PG // ppb
    nc = DEC_COLS
    scale = HEAD_DIM ** -0.5
    q2 = q2_ref[...]
    lane = lax.broadcasted_iota(I32, (nc, LANES), 1)

    @pl.when(s == 0)
    def _():
        gate_sc[...] = jnp.zeros(gate_sc.shape, F32)

    @pl.when(s < n_steps)
    def _():
        g = gate_sc[...]
        for i in range(DEC_PG):
            pidx = s * DEC_PG + i
            sp = _dot(q2, k_refs[i][...].astype(BF16))
            sf = sp[0:nc] + sp[nc:2 * nc]
            s_sc[:, pl.ds(pl.multiple_of(pidx * page, page), page)] = sf
            g = g + jnp.where(lane == pidx // ppb, jnp.sum(sf, axis=1, keepdims=True), 0.0)
        gate_sc[...] = g

    @pl.when(s == n_steps - 1)
    def _():
        gate = jnp.where(lane < nblk, gate_sc[...] * (1.0 / MOBA_BLOCK), -jnp.inf)
        sels = []
        for r in range(MOBA_TOPK):
            mx = jnp.max(gate, axis=1, keepdims=True)
            ix = jnp.min(jnp.where(gate == mx, lane, LANES), axis=1, keepdims=True)
            gate = jnp.where(lane == ix, -jnp.inf, gate)
            sels.append(ix)
        so = _dot_nt(q2, kn_ref[...].astype(BF16))
        so = (so[0:nc] + so[nc:2 * nc]) * scale
        key_c = lax.broadcasted_iota(I32, so.shape, 1)
        q_r = lax.broadcasted_iota(I32, so.shape, 0) // N_HEADS
        own_ok = key_c <= q_r
        so = jnp.where(own_ok, so, NEG)
        npos = s_sc.shape[1]
        blk_of = lax.broadcasted_iota(I32, (nc, npos), 1) // MOBA_BLOCK
        mask = (blk_of == sels[0]) | (blk_of == sels[1]) | (blk_of == sels[2])
        sc = jnp.where(mask, s_sc[...] * scale, NEG)
        m = jnp.maximum(jnp.max(sc, axis=1, keepdims=True), jnp.max(so, axis=1, keepdims=True))
        p = jnp.where(mask, jnp.exp(sc - m), 0.0)
        po = jnp.where(own_ok, jnp.exp(so - m), 0.0)
        s_sc[...] = p
        pown_sc[...] = po
        l_sc[...] = jnp.sum(p, axis=1, keepdims=True) + jnp.sum(po, axis=1, keepdims=True)
        acc_sc[...] = jnp.zeros(acc_sc.shape, F32)

    @pl.when(s >= n_steps)
    def _():
        acc = acc_sc[...]
        for i in range(DEC_PG):
            pidx = (s - n_steps) * DEC_PG + i
            pj = s_sc[:, pl.ds(pl.multiple_of(pidx * page, page), page)].astype(BF16)
            acc = acc + _dot_nt(pj, v_refs[i][...].astype(BF16))
        acc_sc[...] = acc

    @pl.when(s == 2 * n_steps - 1)
    def _():
        acc = acc_sc[...] + _dot(pown_sc[...].astype(BF16), vn_ref[...].astype(BF16))
        out = acc / l_sc[...]
        hrow = lax.broadcasted_iota(I32, (N_HEADS, D_GROUP), 0)
        hlane = lax.broadcasted_iota(I32, (N_HEADS, D_GROUP), 1) // HEAD_DIM
        rows = []
        for q in range(n_q):
            blk8 = out[q * N_HEADS:(q + 1) * N_HEADS, :]
            rows.append(jnp.sum(jnp.where(hrow == hlane, blk8, 0.0), axis=0, keepdims=True))
        rows.append(jnp.zeros((8 - n_q, D_GROUP), F32))
        o_ref[...] = jnp.concatenate(rows, axis=0)


def moba_decode(page_table, q2, k_new, v_new, cache_kt, cache_vt, *, layer, n_q):
    bsz, n_pages = page_table.shape
    page = cache_kt.shape[3]
    assert n_pages % DEC_PG == 0 and MOBA_BLOCK % page == 0 and page % LANES == 0
    assert n_q <= 8 and n_q * N_HEADS <= DEC_COLS and MOBA_TOPK <= n_pages * page // MOBA_BLOCK <= LANES
    n_steps = n_pages // DEC_PG
    kspec = lambda i: pl.BlockSpec(
        (None, None, D_GROUP, page),
        lambda b, s, pt, i=i: (layer, pt[b, jnp.minimum(s, n_steps - 1) * DEC_PG + i], 0, 0))
    vspec = lambda i: pl.BlockSpec(
        (None, None, D_GROUP, page),
        lambda b, s, pt, i=i: (layer, pt[b, jnp.maximum(s - n_steps, 0) * DEC_PG + i], 0, 0))
    perb = lambda r, w: pl.BlockSpec((None, r, w), lambda b, s, pt: (b, 0, 0))
    return pl.pallas_call(
        functools.partial(_moba_decode_kernel, n_steps=n_steps, page=page, n_q=n_q),
        grid_spec=pltpu.PrefetchScalarGridSpec(
            num_scalar_prefetch=1,
            grid=(bsz, 2 * n_steps),
            in_specs=[perb(2 * DEC_COLS, D_GROUP), perb(DEC_NEW, D_GROUP), perb(DEC_NEW, D_GROUP)]
            + [kspec(i) for i in range(DEC_PG)] + [vspec(i) for i in range(DEC_PG)],
            out_specs=perb(8, D_GROUP),
            scratch_shapes=[pltpu.VMEM((DEC_COLS, n_pages * page), F32), pltpu.VMEM((DEC_COLS, LANES), F32),
                            pltpu.VMEM((DEC_COLS, D_GROUP), F32), pltpu.VMEM((DEC_COLS, 1), F32),
                            pltpu.VMEM((DEC_COLS, DEC_NEW), F32)]),
        out_shape=jax.ShapeDtypeStruct((bsz, 8, D_GROUP), F32),
        compiler_params=_cparams(("arbitrary", "arbitrary"), VMEM_LIMIT),
        name="moba_decode",
    )(page_table, q2, k_new, v_new, *([cache_kt] * DEC_PG), *([cache_vt] * DEC_PG))


def _log_sigmoid(x):
    return jnp.minimum(x, 0.0) - jnp.log1p(jnp.exp(-jnp.abs(x)))


def _mlstm_kernel(qk_ref, v_ref, om_ref, g_ref, cw_ref, cb_ref, bg_ref, gh_ref,
                  conv0_ref, c0_ref, n0_ref, m0_ref,
                  h_ref, cout_ref, nout_ref, mout_ref,
                  conv_sc, c_sc, n_sc, m_sc, *, L, t_valid):
    ci = pl.program_id(1)
    nh, dh, dg = N_HEADS, HEAD_DIM, D_GROUP

    @pl.when(ci == 0)
    def _():
        conv_sc[...] = conv0_ref[...]
        c_sc[...] = c0_ref[...]
        n_sc[...] = n0_ref[...]
        m_sc[...] = m0_ref[...]

    x = qk_ref[...]
    xc = jnp.concatenate([conv_sc[...], x], axis=0)
    y = cb_ref[...]
    for j in range(CONV_W):
        off = 8 - (CONV_W - 1) + j
        y = y + xc[off:off + L] * cw_ref[j:j + 1, :]
    conv_sc[...] = x[L - 8:L]
    qk = y * jax.nn.sigmoid(y)
    q_all = qk[:, :dg].astype(BF16)
    k_all = qk[:, dg:] * (dh ** -0.5)
    v_all = v_ref[...]
    om = om_ref[...]

    g = g_ref[...] + bg_ref[...]
    lf = _log_sigmoid(g)
    ri = lax.broadcasted_iota(I32, (L, L), 0)
    cj = lax.broadcasted_iota(I32, (L, L), 1)
    causal = cj <= ri
    tril = jnp.where(causal, 1.0, 0.0).astype(BF16)
    lf3 = _split3(lf)
    b_col = _dot(tril, lf3[0]) + _dot(tril, lf3[1]) + _dot(tril, lf3[2])
    sel_r = lax.broadcasted_iota(I32, (8, LANES), 0)
    sel_c = lax.broadcasted_iota(I32, (8, LANES), 1)
    sel_i = jnp.where(sel_c == sel_r, 1.0, 0.0).astype(BF16)
    sel_f = jnp.where(sel_c == sel_r + nh, 1.0, 0.0).astype(BF16)

    def rows_of(sel, a):
        a3 = _split3(a)
        return _dot_nt(sel, a3[0]) + _dot_nt(sel, a3[1]) + _dot_nt(sel, a3[2])

    li_row = rows_of(sel_i, g)
    b_row = rows_of(sel_f, b_col)
    rvalid = lax.broadcasted_iota(I32, (L, 1), 0) < t_valid
    outs = []
    for h in range(nh):
        hs = slice(h * dh, (h + 1) * dh)
        bc = b_col[:, nh + h:nh + h + 1]
        lic = g[:, h:h + 1]
        log_d = jnp.where(causal, bc - b_row[h:h + 1, :] + li_row[h:h + 1, :], -jnp.inf)
        m_prev = m_sc[h:h + 1, 0:1]
        log_inter = m_prev + bc
        m_t = jnp.maximum(log_inter, jnp.max(log_d, axis=-1, keepdims=True))
        dw = jnp.exp(log_d - m_t)
        w_inter = jnp.exp(log_inter - m_t)
        qh = q_all[:, hs]
        kh = k_all[:, hs]
        vh = v_all[:, hs]
        c_h = c_sc[h]
        n_h = n_sc[h:h + 1, :]
        s = _dot_nt(qh, kh.astype(BF16)) * dw
        num = _dot(s.astype(BF16), vh) + w_inter * _dot(qh, c_h.astype(BF16))
        qn = jnp.sum(qh.astype(F32) * n_h, axis=-1, keepdims=True)
        den = jnp.sum(s, axis=-1, keepdims=True) + w_inter * qn
        hh = num / jnp.maximum(jnp.abs(den), jnp.exp(-m_t))
        m_new = m_t[t_valid - 1:t_valid, :]
        b_last = bc[t_valid - 1:t_valid, :]
        w_state = jnp.where(rvalid, jnp.exp(b_last - bc + lic - m_new), 0.0)
        decay = jnp.exp(m_prev + b_last - m_new)
        kw = kh * w_state
        c_sc[h] = decay * c_h + lax.dot_general(kw.astype(BF16), vh, _TN, preferred_element_type=F32)
        n_sc[h:h + 1, :] = decay * n_h + jnp.sum(kw, axis=0, keepdims=True)
        m_sc[h:h + 1, :] = jnp.broadcast_to(m_new, (1, LANES))
        hn = hh * lax.rsqrt(jnp.mean(hh * hh, axis=-1, keepdims=True) + EPS) * gh_ref[:, hs]
        outs.append(hn * jax.nn.sigmoid(om[:, hs]))
    h_ref[...] = jnp.concatenate(outs, axis=-1).astype(h_ref.dtype)
    cout_ref[...] = c_sc[...]
    nout_ref[...] = n_sc[...]
    mout_ref[...] = m_sc[...]


def mlstm(qk, vm, om, gates, conv_w, conv_b, bg_pad, g_head, conv0, c0, n0, m0b, *, L, t_valid):
    b, t, _ = qk.shape
    nh, dh, dg = N_HEADS, HEAD_DIM, D_GROUP
    nc = t // L
    tok = lambda w: pl.BlockSpec((None, L, w), lambda i, c: (i, c, 0))
    full2 = lambda r, w: pl.BlockSpec((r, w), lambda i, c: (0, 0))
    return pl.pallas_call(
        functools.partial(_mlstm_kernel, L=L, t_valid=t_valid),
        grid=(b, nc),
        in_specs=[tok(2 * dg), tok(dg), tok(dg), tok(GATE_PAD),
                  full2(CONV_W, 2 * dg), full2(1, 2 * dg), full2(1, GATE_PAD), full2(1, dg),
                  pl.BlockSpec((None, 8, 2 * dg), lambda i, c: (i, 0, 0)),
                  pl.BlockSpec((None, nh, dh, dh), lambda i, c: (i, 0, 0, 0)),
                  pl.BlockSpec((None, nh, dh), lambda i, c: (i, 0, 0)),
                  pl.BlockSpec((None, nh, LANES), lambda i, c: (i, 0, 0))],
        out_specs=[tok(dg),
                   pl.BlockSpec((None, nh, dh, dh), lambda i, c: (i, 0, 0, 0)),
                   pl.BlockSpec((None, nh, dh), lambda i, c: (i, 0, 0)),
                   pl.BlockSpec((None, nh, LANES), lambda i, c: (i, 0, 0))],
        out_shape=[jax.ShapeDtypeStruct((b, t, dg), BF16),
                   jax.ShapeDtypeStruct((b, nh, dh, dh), F32),
                   jax.ShapeDtypeStruct((b, nh, dh), F32),
                   jax.ShapeDtypeStruct((b, nh, LANES), F32)],
        scratch_shapes=[pltpu.VMEM((8, 2 * dg), F32), pltpu.VMEM((nh, dh, dh), F32),
                        pltpu.VMEM((nh, dh), F32), pltpu.VMEM((nh, LANES), F32)],
        compiler_params=_cparams(("arbitrary", "arbitrary"), VMEM_LIMIT),
        name="mlstm",
    )(qk, vm, om, gates, conv_w, conv_b, bg_pad, g_head, conv0, c0, n0, m0b)


def _outproj_kernel(att_ref, hm_ref, x_ref, gt_ref, sc_ref, sh_ref, g_ref, wo_ref, wrh_ref, wrl_ref, br_ref,
                    *rest):
    xo_ref, h2_ref, ti_ref, tw_ref = rest[-4:]
    dg = D_GROUP
    mix = _dot(att_ref[...], wo_ref[0:dg, :]) + _dot(hm_ref[...], wo_ref[dg:2 * dg, :])
    xn = x_ref[...] + gt_ref[...] * mix
    xo_ref[...] = xn
    ms = jnp.mean(xn * xn, axis=-1, keepdims=True)
    y = xn * lax.rsqrt(ms + EPS) * g_ref[...]
    h2 = y * (1.0 + sc_ref[...]) + sh_ref[...]
    hh, hl = _split2(h2)
    h2_ref[...] = hh.astype(F32)
    wrh = wrh_ref[...]
    logits = _dot(hh, wrh) + _dot(hh, wrl_ref[...]) + _dot(hl, wrh) + br_ref[...]
    col = lax.broadcasted_iota(I32, logits.shape, 1)
    lg = jnp.where(col < N_EXPERTS, logits, -jnp.inf)
    vals, idxs = [], []
    for _ in range(TOP_K):
        mx = jnp.max(lg, axis=-1, keepdims=True)
        ix = jnp.min(jnp.where(lg == mx, col, LANES), axis=-1, keepdims=True)
        lg = jnp.where(col == ix, -jnp.inf, lg)
        vals.append(mx)
        idxs.append(ix)
    es = [jnp.exp(v - vals[0]) for v in vals]
    den = es[0] + es[1] + es[2] + es[3]
    ti = jnp.zeros(logits.shape, I32)
    tw = jnp.zeros(logits.shape, F32)
    for r in range(TOP_K):
        ti = jnp.where(col == r, idxs[r], ti)
        tw = jnp.where(col == r, es[r] / den, tw)
    ti_ref[...] = ti
    tw_ref[...] = tw


def out_proj(att, hm, x, mod, g_ffn, w_out, wr_hi, wr_lo, br_pad, routed, *, n_routed, row0, tm, tiles_per_group):
    n, d = x.shape
    rm = mod.shape[1]
    dg = D_GROUP
    b0 = row0 // tm
    row = lambda i: (i, 0)
    rrow = lambda i: (b0 + i, 0)
    modspec = lambda c: pl.BlockSpec((None, rm, d), lambda i, c=c: (i // tiles_per_group, 0, c))
    const = lambda r, w: pl.BlockSpec((r, w), lambda i: (0, 0))
    in_specs = [pl.BlockSpec((tm, dg), row), pl.BlockSpec((tm, dg), row), pl.BlockSpec((tm, d), row),
                modspec(2), modspec(4), modspec(3),
                const(1, d), const(2 * dg, d), const(d, LANES), const(d, LANES), const(1, LANES)]
    args = [att, hm, x, mod, mod, mod, g_ffn, w_out, wr_hi, wr_lo, br_pad]
    aliases = {}
    if routed is not None:
        aliases = {len(args) + j: 1 + j for j in range(3)}
        in_specs += [pl.BlockSpec(memory_space=pl.ANY)] * 3
        args += list(routed)
    return pl.pallas_call(
        _outproj_kernel,
        grid=(n // tm,),
        in_specs=in_specs,
        out_specs=[pl.BlockSpec((tm, d), row), pl.BlockSpec((tm, d), rrow),
                   pl.BlockSpec((tm, LANES), rrow), pl.BlockSpec((tm, LANES), rrow)],
        out_shape=[jax.ShapeDtypeStruct((n, d), F32), jax.ShapeDtypeStruct((n_routed, d), F32),
                   jax.ShapeDtypeStruct((n_routed, LANES), I32), jax.ShapeDtypeStruct((n_routed, LANES), F32)],
        input_output_aliases=aliases,
        compiler_params=_cparams(("arbitrary",), VMEM_LIMIT),
        name="out_proj",
    )(*args)


def _rank_kernel(ti_ref, rank_ref, cnt_ref, base_sc):
    i = pl.program_id(0)

    @pl.when(i == 0)
    def _():
        base_sc[...] = jnp.zeros(base_sc.shape, F32)

    ti = ti_ref[...]
    tm = ti.shape[0]
    col = lax.broadcasted_iota(I32, ti.shape, 1)
    es = []
    oh = jnp.zeros(ti.shape, F32)
    for k in range(TOP_K):
        ek = jnp.sum(jnp.where(col == k, ti, 0), axis=-1, keepdims=True)
        es.append(ek)
        oh = oh + jnp.where(col == ek, 1.0, 0.0)
    ri = lax.broadcasted_iota(I32, (tm, tm), 0)
    ci = lax.broadcasted_iota(I32, (tm, tm), 1)
    tri = jnp.where(ci < ri, 1.0, 0.0).astype(BF16)
    rank = _dot(tri, oh.astype(BF16)) + base_sc[0:1, :]
    out = jnp.zeros(ti.shape, F32)
    for k in range(TOP_K):
        rk = jnp.sum(jnp.where(col == es[k], rank, 0.0), axis=-1, keepdims=True)
        out = jnp.where(col == k, rk, out)
    rank_ref[...] = out.astype(I32)
    tot = base_sc[0:1, :] + jnp.sum(oh, axis=0, keepdims=True)
    base_sc[...] = jnp.broadcast_to(tot, base_sc.shape)
    cnt_ref[...] = base_sc[...].astype(I32)


def expert_ranks(ti, *, tm):
    n = ti.shape[0]
    return pl.pallas_call(
        _rank_kernel,
        grid=(n // tm,),
        in_specs=[pl.BlockSpec((tm, LANES), lambda i: (i, 0))],
        out_specs=[pl.BlockSpec((tm, LANES), lambda i: (i, 0)), pl.BlockSpec((8, LANES), lambda i: (0, 0))],
        out_shape=[jax.ShapeDtypeStruct((n, LANES), I32), jax.ShapeDtypeStruct((8, LANES), I32)],
        scratch_shapes=[pltpu.VMEM((8, LANES), F32)],
        compiler_params=_cparams(("arbitrary",)),
        name="expert_ranks",
    )(ti)


def _moe_kernel(be_ref, first_ref, nused_ref, x_ref, wgu_ref, bgu_ref, wd_ref, bd_ref, y_ref, wgu_sc, wd_sc):
    i = pl.program_id(0)
    f = wd_ref.shape[0]

    @pl.when(first_ref[i] == 1)
    def _():
        wgu_sc[...] = wgu_ref[...].astype(BF16)
        wd_sc[...] = wd_ref[...].astype(BF16)

    @pl.when(i < nused_ref[0])
    def _():
        gu = _dot(x_ref[...].astype(BF16), wgu_sc[...]) + bgu_ref[...]
        gg = jnp.minimum(gu[:, :f], SWIGLU_LIMIT)
        uu = jnp.clip(gu[:, f:], -SWIGLU_LIMIT, SWIGLU_LIMIT)
        act = (uu + 1.0) * gg * jax.nn.sigmoid(gg * SWIGLU_ALPHA)
        y_ref[...] = _dot(act.astype(BF16), wd_sc[...]) + bd_ref[...]


def moe_mlp(blk_exp, blk_first, n_used, xg, w_gu, b_gu, w_down, b_down, *, layer, bm):
    m, d = xg.shape
    depth, e, _, f2 = w_gu.shape
    f = f2 // 2
    nblk = m // bm
    xmap = lambda i, be, fi, nu: (jnp.minimum(i, nu[0] - 1), 0)
    wmap = lambda i, be, fi, nu: (layer, be[i], 0, 0)
    return pl.pallas_call(
        _moe_kernel,
        grid_spec=pltpu.PrefetchScalarGridSpec(
            num_scalar_prefetch=3,
            grid=(nblk,),
            in_specs=[pl.BlockSpec((bm, d), xmap),
                      pl.BlockSpec((None, None, d, f2), wmap),
                      pl.BlockSpec((None, None, 1, f2), wmap),
                      pl.BlockSpec((None, None, f, d), wmap),
                      pl.BlockSpec((None, None, 1, d), wmap)],
            out_specs=pl.BlockSpec((bm, d), xmap),
            scratch_shapes=[pltpu.VMEM((d, f2), BF16), pltpu.VMEM((f, d), BF16)]),
        out_shape=jax.ShapeDtypeStruct((m, d), F32),
        compiler_params=_cparams(("arbitrary",), VMEM_LIMIT),
        name="moe_mlp",
    )(blk_exp, blk_first, n_used, xg, w_gu, b_gu.reshape(depth, e, 1, f2), w_down, b_down.reshape(depth, e, 1, d))


def _combine_kernel(y_ref, tw_ref, x_ref, gt_ref, o_ref):
    tw = tw_ref[...]
    ff = y_ref[0] * tw[:, 0:1]
    for k in range(1, TOP_K):
        ff = ff + y_ref[k] * tw[:, k:k + 1]
    o_ref[...] = x_ref[...] + gt_ref[...] * ff


def moe_combine(yg, tw, x, mod, *, row0, tm, tiles_per_group):
    n, d = x.shape
    rm = mod.shape[1]
    b0 = row0 // tm
    return pl.pallas_call(
        _combine_kernel,
        grid=(n // tm,),
        in_specs=[pl.BlockSpec((TOP_K, tm, d), lambda i: (0, b0 + i, 0)),
                  pl.BlockSpec((tm, LANES), lambda i: (b0 + i, 0)),
                  pl.BlockSpec((tm, d), lambda i: (i, 0)),
                  pl.BlockSpec((None, rm, d), lambda i: (i // tiles_per_group, 0, N_ADA - 1))],
        out_specs=pl.BlockSpec((tm, d), lambda i: (i, 0)),
        out_shape=jax.ShapeDtypeStruct((n, d), F32),
        compiler_params=_cparams(("arbitrary",), VMEM_LIMIT),
        name="moe_combine",
    )(yg, tw, x, mod)


def _rms_kernel(x_ref, g_ref, o_ref):
    x = x_ref[...]
    o_ref[...] = x * lax.rsqrt(jnp.mean(x * x, axis=-1, keepdims=True) + EPS) * g_ref[...]


def final_norm(x, g, *, tm):
    n, d = x.shape
    return pl.pallas_call(
        _rms_kernel,
        grid=(n // tm,),
        in_specs=[pl.BlockSpec((tm, d), lambda i: (i, 0)), pl.BlockSpec((1, d), lambda i: (0, 0))],
        out_specs=pl.BlockSpec((tm, d), lambda i: (i, 0)),
        out_shape=jax.ShapeDtypeStruct((n, d), F32),
        compiler_params=_cparams(("arbitrary",)),
        name="final_norm",
    )(x, g)


def _decode_q2(q):
    b, ts, _ = q.shape
    q4 = q.reshape(b, ts, N_HEADS, HEAD_DIM)
    eye = jnp.eye(N_HEADS, dtype=F32)
    qbd = q4[:, :, None, :, :] * eye[None, None, :, :, None]
    qbd = qbd.reshape(b, ts * N_HEADS, D_GROUP)
    qbd = jnp.pad(qbd, ((0, 0), (0, DEC_COLS - ts * N_HEADS), (0, 0)))
    hi = qbd.astype(BF16)
    lo = (qbd - hi.astype(F32)).astype(BF16)
    return jnp.concatenate([hi, lo], axis=1)


def _route(ti, bm):
    n = ti.shape[0]
    na = n * TOP_K
    tm = max(t for t in range(8, 513, 8) if n % t == 0)
    rank, cnt = expert_ranks(ti, tm=tm)
    counts = cnt[0, :N_EXPERTS]
    padded = ((counts + bm - 1) // bm) * bm
    pad_end = jnp.cumsum(padded)
    pad_start = pad_end - padded
    e4 = ti[:, :TOP_K]
    onehot = e4[:, :, None] == jnp.arange(N_EXPERTS, dtype=I32)[None, None, :]
    pos = jnp.sum(jnp.where(onehot, pad_start[None, None, :], 0), axis=-1) + rank[:, :TOP_K]
    nblk = -(-(na + N_EXPERTS * (bm - 1)) // bm)
    m = nblk * bm
    tok = jnp.broadcast_to(jnp.arange(n, dtype=I32)[:, None], (n, TOP_K))
    row_tok = jnp.zeros((m,), I32).at[pos.reshape(-1)].set(tok.reshape(-1), unique_indices=True)
    blk_start = jnp.arange(nblk, dtype=I32) * bm
    blk_exp = jnp.minimum(jnp.sum(blk_start[:, None] >= pad_end[None, :], axis=1), N_EXPERTS - 1).astype(I32)
    n_used = (pad_end[-1] // bm).astype(I32)
    blk_first = jnp.concatenate([jnp.ones((1,), I32), (blk_exp[1:] != blk_exp[:-1]).astype(I32)])
    blk_first = jnp.where(jnp.arange(nblk) < n_used, blk_first, 0).astype(I32)
    return row_tok, blk_exp, blk_first, n_used.reshape(1), pos


def _layer(l, depth, xp, xs, modp, mods, P, S, W, stacks):
    dg = D_GROUP
    Bp, T = P["batch"], P["seq"]
    Bs, Ts = S["batch"], S["seq"]
    d = xp.shape[1]
    tm = 512
    kst_p, vst_p = stacks
    qa, kst_p, vst_p, kb, vbt, qk, vm, om, gt, kmean = in_proj(
        xp, modp, W["g_mix"][l], W["w_in"][l], kst_p, vst_p, layer=l, depth=depth,
        tm=tm, tiles_per_group=T // tm, fresh=True)
    att_p = moba_prompt(qa, kb, vbt, kmean.reshape(-1, dg), batch=Bp, seq=T)
    Lp = 256
    hm_p, c_p, n_p, m_p = mlstm(
        qk.reshape(Bp, T, 2 * dg), vm.reshape(Bp, T, dg), om.reshape(Bp, T, dg), gt.reshape(Bp, T, GATE_PAD),
        W["conv_w"][l], W["conv_b"][l], W["bg_pad"][l], W["g_head"][l],
        jnp.zeros((Bp, 8, 2 * dg), F32), jnp.zeros((Bp, N_HEADS, HEAD_DIM, HEAD_DIM), F32),
        jnp.zeros((Bp, N_HEADS, HEAD_DIM), F32), jnp.zeros((Bp, N_HEADS, LANES), F32), L=Lp, t_valid=Lp)
    conv_p = qk.reshape(Bp, T, 2 * dg)[:, T - (CONV_W - 1):]
    np_ = Bp * T
    ntot = np_ + Bs * Ts
    xp1, *routed = out_proj(att_p, hm_p.reshape(np_, dg), xp, modp, W["g_ffn"][l], W["w_out"][l],
                            W["wr_hi"][l], W["wr_lo"][l], W["br_pad"][l], None,
                            n_routed=ntot, row0=0, tm=tm, tiles_per_group=T // tm)
    ns = Bs * Ts
    qa_s, ka_s, va_s, _, _, qk_s, vm_s, om_s, gt_s = in_proj(
        xs, mods, W["g_mix"][l], W["w_in"][l], None, None, layer=l, depth=depth,
        tm=ns, tiles_per_group=1, fresh=False)
    newpad = lambda a: jnp.pad(a.reshape(Bs, Ts, dg), ((0, 0), (0, DEC_NEW - Ts), (0, 0)))
    att_s = moba_decode(S["page_table"], _decode_q2(qa_s.reshape(Bs, Ts, dg)), newpad(ka_s), newpad(va_s),
                        S["cache_kt"], S["cache_vt"], layer=l, n_q=Ts)
    att_s = att_s[:, :Ts].reshape(ns, dg).astype(BF16)
    Ls = 128
    padt = lambda a: jnp.pad(a.reshape(Bs, Ts, -1), ((0, 0), (0, Ls - Ts), (0, 0)))
    conv0 = jnp.pad(S["state_conv"][l], ((0, 0), (8 - (CONV_W - 1), 0), (0, 0)))
    m0b = jnp.broadcast_to(S["state_m"][l][:, :, None], (Bs, N_HEADS, LANES))
    hm_s, c_s, n_s, m_s = mlstm(padt(qk_s), padt(vm_s), padt(om_s), padt(gt_s),
                                W["conv_w"][l], W["conv_b"][l], W["bg_pad"][l], W["g_head"][l],
                                conv0, S["state_c"][l], S["state_n"][l], m0b, L=Ls, t_valid=Ts)
    hm_s = hm_s[:, :Ts].reshape(ns, dg)
    conv_s = qk_s.reshape(Bs, Ts, 2 * dg)[:, Ts - (CONV_W - 1):]
    xs1, h2, ti, tw = out_proj(att_s, hm_s, xs, mods, W["g_ffn"][l], W["w_out"][l],
                               W["wr_hi"][l], W["wr_lo"][l], W["br_pad"][l], routed,
                               n_routed=ntot, row0=np_, tm=ns, tiles_per_group=1)
    row_tok, blk_exp, blk_first, n_used, pos = _route(ti, MOE_BM)
    xg = h2[row_tok]
    yb = moe_mlp(blk_exp, blk_first, n_used, xg, W["w_gu"], W["b_gu"], W["w_down"], W["b_down"], layer=l, bm=MOE_BM)
    yg = yb[pos.T.reshape(-1)].reshape(TOP_K, ntot, d)
    xp2 = moe_combine(yg, tw, xp1, modp, row0=0, tm=tm, tiles_per_group=T // tm)
    xs2 = moe_combine(yg, tw, xs1, mods, row0=np_, tm=ns, tiles_per_group=1)
    outs_p = (c_p, n_p, m_p[:, :, 0], conv_p)
    outs_s = (ka_s.reshape(Bs, Ts, N_HEADS, HEAD_DIM), va_s.reshape(Bs, Ts, N_HEADS, HEAD_DIM),
              c_s, n_s, m_s[:, :, 0], conv_s)
    return xp2, xs2, outs_p, outs_s, (kst_p, vst_p)


def kernel(x_prompt, x_sample, c_prompt, c_sample, cache_k, cache_v, page_table, state_c, state_n, state_m,
           state_conv, w_ada, b_ada, g_mix, w_in, b_gate, conv_w, conv_b, g_head, w_out, g_ffn, w_router,
           b_router, w_gu, b_gu, w_down, b_down, g_final):
    depth = w_ada.shape[0]
    Bp, T, d = x_prompt.shape
    Bs, Ts, _ = x_sample.shape
    dg = D_GROUP
    n_gate = 2 * N_HEADS
    w_in_pad = jnp.pad(w_in, ((0, 0), (0, 0), (0, GATE_PAD - n_gate))).astype(BF16)
    wr_pad = jnp.pad(w_router, ((0, 0), (0, 0), (0, LANES - N_EXPERTS)))
    wr_hi = wr_pad.astype(BF16)
    wr_lo = (wr_pad - wr_hi.astype(F32)).astype(BF16)
    W = dict(
        g_mix=g_mix.reshape(depth, 1, d), w_in=w_in_pad,
        conv_w=conv_w, conv_b=conv_b.reshape(depth, 1, 2 * dg),
        bg_pad=jnp.pad(b_gate, ((0, 0), (0, GATE_PAD - n_gate))).reshape(depth, 1, GATE_PAD),
        g_head=g_head.reshape(depth, 1, dg), w_out=w_out.astype(BF16), g_ffn=g_ffn.reshape(depth, 1, d),
        wr_hi=wr_hi, wr_lo=wr_lo,
        br_pad=jnp.pad(b_router, ((0, 0), (0, LANES - N_EXPERTS))).reshape(depth, 1, LANES),
        w_gu=w_gu, b_gu=b_gu, w_down=w_down, b_down=b_down)
    n_pool, page = cache_k.shape[1], cache_k.shape[2]
    assert (page_table.shape[1] * page) % MOBA_BLOCK == 0
    pages_t = lambda c: c.transpose(0, 1, 3, 4, 2).reshape(depth, n_pool, dg, page)
    S = dict(batch=Bs, seq=Ts, page_table=page_table, cache_kt=pages_t(cache_k), cache_vt=pages_t(cache_v),
             state_c=state_c, state_n=state_n, state_m=state_m, state_conv=state_conv)
    P = dict(batch=Bp, seq=T)
    nc = Bp + Bs
    rpad = -(-nc // 8) * 8
    c_all = jnp.pad(jnp.concatenate([c_prompt, c_sample], axis=0), ((0, rpad - nc), (0, 0)))
    mod = ada_mod(c_all, w_ada, b_ada)
    xp = x_prompt.reshape(Bp * T, d)
    xs = x_sample.reshape(Bs * Ts, d)
    acc_p, acc_s = [], []
    stacks = (None, None)
    for l in range(depth):
        modp = mod[l, :Bp].reshape(Bp, 1, N_ADA * d)
        mods = jnp.repeat(mod[l, Bp:nc], Ts, axis=0).reshape(1, Bs * Ts, N_ADA * d)
        xp, xs, op, os_, stacks = _layer(l, depth, xp, xs, modp, mods, P, S, W, stacks)
        acc_p.append(op)
        acc_s.append(os_)
    y_p = final_norm(xp, g_final.reshape(1, d), tm=512).reshape(Bp, T, d)
    y_s = final_norm(xs, g_final.reshape(1, d), tm=Bs * Ts).reshape(Bs, Ts, d)
    stack = lambda acc, i: jnp.stack([a[i] for a in acc])
    rows = lambda a: a.reshape(depth, Bp, N_HEADS, HEAD_DIM, T).transpose(0, 1, 4, 2, 3)
    return ((y_p, y_s, rows(stacks[0]), rows(stacks[1])) + tuple(stack(acc_p, i) for i in range(4))
            + tuple(stack(acc_s, i) for i in range(6)))
```

```python
import functools

import jax
import jax.numpy as jnp
from jax import lax
from jax.experimental import pallas as pl
from jax.experimental.pallas import tpu as pltpu

F32 = jnp.float32
BF16 = jnp.bfloat16
I32 = jnp.int32
U32 = jnp.uint32

HEAD_DIM = 64
N_HEADS = 8
D_GROUP = N_HEADS * HEAD_DIM
MOBA_BLOCK = 256
MOBA_TOPK = 3
CONV_W = 4
N_EXPERTS = 32
TOP_K = 4
SWIGLU_ALPHA = 1.702
SWIGLU_LIMIT = 7.0
N_ADA = 6
EPS = 1e-6
NEG = -1e30
LANES = 128
GATE_PAD = LANES
MOE_BM = 512
VMEM_LIMIT = 56 * 1024 * 1024


def _cparams(sem, vmem=None):
    return pltpu.CompilerParams(dimension_semantics=sem, vmem_limit_bytes=vmem)


def _split2(x):
    hi = x.astype(BF16)
    lo = (x - hi.astype(F32)).astype(BF16)
    return hi, lo


def _split3(x):
    hi = x.astype(BF16)
    r = x - hi.astype(F32)
    mid = r.astype(BF16)
    lo = (r - mid.astype(F32)).astype(BF16)
    return hi, mid, lo


_NT = (((1,), (1,)), ((), ()))
_TN = (((0,), (0,)), ((), ()))


def _dot(a, b):
    return jnp.dot(a, b, preferred_element_type=F32)


def _dot_nt(a, b):
    return lax.dot_general(a, b, _NT, preferred_element_type=F32)


def _ada_kernel(c_ref, w_ref, b_ref, o_ref):
    c = c_ref[...]
    s = (c * jax.nn.sigmoid(c)).astype(BF16)
    o_ref[...] = _dot(s, w_ref[...].astype(BF16)) + b_ref[...]


def ada_mod(c_all, w_ada, b_ada):
    depth, d, nd = w_ada.shape
    r = c_all.shape[0]
    nj = nd // d
    return pl.pallas_call(
        _ada_kernel,
        grid=(depth, nj),
        in_specs=[pl.BlockSpec((r, d), lambda l, j: (0, 0)),
                  pl.BlockSpec((None, d, d), lambda l, j: (l, 0, j)),
                  pl.BlockSpec((None, 1, d), lambda l, j: (l, 0, j))],
        out_specs=pl.BlockSpec((None, r, d), lambda l, j: (l, 0, j)),
        out_shape=jax.ShapeDtypeStruct((depth, r, nd), F32),
        compiler_params=_cparams(("arbitrary", "arbitrary")),
        name="ada_mod",
    )(c_all, w_ada, b_ada.reshape(depth, 1, nd))


def _inproj_kernel(x_ref, g_ref, sc_ref, sh_ref, w_ref, *rest, fresh, aliased):
    if aliased:
        rest = rest[2:]
    qa_ref, ks_ref, vs_ref, kb_ref, vb_ref, qk_ref, vm_ref, om_ref, gt_ref = rest[:9]
    x = x_ref[...]
    ms = jnp.mean(x * x, axis=-1, keepdims=True)
    y = x * lax.rsqrt(ms + EPS) * g_ref[...]
    h = (y * (1.0 + sc_ref[...]) + sh_ref[...]).astype(BF16)
    dg = D_GROUP
    tm = x.shape[0]

    def seg(a, b):
        return _dot(h, w_ref[:, a:b])

    qa_ref[...] = seg(0, dg)
    ka = seg(dg, 2 * dg)
    va = seg(2 * dg, 3 * dg)
    kb_ref[...] = ka.astype(BF16)
    if fresh:
        ks_ref[...] = ka.T
        vt = va.T
        vs_ref[...] = vt
        vb_ref[...] = vt.astype(BF16)
        km_ref = rest[9]
        nblk = tm // MOBA_BLOCK
        rows = [jnp.sum(ka[i * MOBA_BLOCK:(i + 1) * MOBA_BLOCK], axis=0, keepdims=True) for i in range(nblk)]
        km_ref[...] = jnp.concatenate(rows, axis=0) * (1.0 / MOBA_BLOCK)
    else:
        ks_ref[...] = ka
        vs_ref[...] = va
        vb_ref[...] = va.astype(BF16)
    qk_ref[...] = seg(3 * dg, 5 * dg)
    vm_ref[...] = seg(5 * dg, 6 * dg).astype(BF16)
    om_ref[...] = seg(6 * dg, 7 * dg)
    gt_ref[...] = seg(7 * dg, 7 * dg + GATE_PAD)


def in_proj(x, mod, g, w_pad, kstack, vstack, *, layer, depth, tm, tiles_per_group, fresh):
    n, d = x.shape
    rm = mod.shape[1]
    nw = w_pad.shape[1]
    dg = D_GROUP
    grid = (n // tm,)
    row = lambda i: (i, 0)
    modspec = lambda c: pl.BlockSpec((None, rm, d), lambda i, c=c: (i // tiles_per_group, 0, c))
    if fresh:
        ngrp = n // (tm * tiles_per_group)
        seq = tm * tiles_per_group
        tmap = lambda i: (i // tiles_per_group, 0, i % tiles_per_group)
        kv_shape = jax.ShapeDtypeStruct((depth, ngrp, dg, seq), F32)
        kv_spec = pl.BlockSpec((None, None, dg, tm), lambda i: (layer,) + tmap(i))
        vb_shape = jax.ShapeDtypeStruct((ngrp, dg, seq), BF16)
        vb_spec = pl.BlockSpec((None, dg, tm), tmap)
    else:
        kv_shape = jax.ShapeDtypeStruct((n, dg), F32)
        kv_spec = pl.BlockSpec((tm, dg), row)
        vb_shape = jax.ShapeDtypeStruct((n, dg), BF16)
        vb_spec = pl.BlockSpec((tm, dg), row)
    out_shapes = [jax.ShapeDtypeStruct((n, dg), F32), kv_shape, kv_shape,
                  jax.ShapeDtypeStruct((n, dg), BF16), vb_shape,
                  jax.ShapeDtypeStruct((n, 2 * dg), F32), jax.ShapeDtypeStruct((n, dg), BF16),
                  jax.ShapeDtypeStruct((n, dg), F32), jax.ShapeDtypeStruct((n, GATE_PAD), F32)]
    out_specs = [pl.BlockSpec((tm, dg), row), kv_spec, kv_spec,
                 pl.BlockSpec((tm, dg), row), vb_spec,
                 pl.BlockSpec((tm, 2 * dg), row), pl.BlockSpec((tm, dg), row),
                 pl.BlockSpec((tm, dg), row), pl.BlockSpec((tm, GATE_PAD), row)]
    if fresh:
        nblk = tm // MOBA_BLOCK
        out_shapes.append(jax.ShapeDtypeStruct((n // tm, nblk, dg), F32))
        out_specs.append(pl.BlockSpec((None, nblk, dg), lambda i: (i, 0, 0)))
    in_specs = [pl.BlockSpec((tm, d), row),
                pl.BlockSpec((1, d), lambda i: (0, 0)),
                modspec(1), modspec(0),
                pl.BlockSpec((d, nw), lambda i: (0, 0))]
    args = [x, g, mod, mod, w_pad]
    aliased = kstack is not None
    aliases = {}
    if aliased:
        in_specs += [pl.BlockSpec(memory_space=pl.ANY)] * 2
        args += [kstack, vstack]
        aliases = {5: 1, 6: 2}
    return pl.pallas_call(
        functools.partial(_inproj_kernel, fresh=fresh, aliased=aliased),
        grid=grid,
        in_specs=in_specs,
        out_specs=out_specs,
        out_shape=out_shapes,
        input_output_aliases=aliases,
        compiler_params=_cparams(("arbitrary",), VMEM_LIMIT),
        name="in_proj",
    )(*args)


L_ROWS = 16
MOBA_G = 4


def _moba_kernel(q_ref, k_ref, v_ref, km_ref, o_ref, vt_sc, qb_sc, m_sc, acc_sc, sel_sc, *, tq, seq):
    qi = pl.program_id(2)
    nb = km_ref.shape[0]
    blk = MOBA_BLOCK
    nd = LANES + L_ROWS
    G = MOBA_G
    kb_sc = k_ref

    @pl.when(qi == 0)
    def _():
        vt_sc[0:LANES, :] = v_ref[...]
        vt_sc[LANES:nd, :] = jnp.ones((L_ROWS, seq), BF16)

    qt = q_ref[...].T
    drow = lax.broadcasted_iota(I32, (LANES, tq), 0)
    head_rows = [drow < HEAD_DIM, drow >= HEAD_DIM]
    kmh, kml = _split2(km_ref[...])
    bidx = lax.broadcasted_iota(I32, (nb, tq), 0)
    key_i = lax.broadcasted_iota(I32, (blk, tq), 0)
    qry_i = lax.broadcasted_iota(I32, (blk, tq), 1)
    causal = key_i <= qry_i
    start = pl.multiple_of(qi * blk, blk)
    kd = kb_sc[pl.ds(start, blk), :]
    vd = vt_sc[:, pl.ds(start, blk)]
    for h in range(2):
        qh = jnp.where(head_rows[h], qt, 0.0)
        qhi, qlo = _split2(qh)
        gate = _dot(kmh, qhi) + _dot(kmh, qlo) + _dot(kml, qhi)
        gate = jnp.where(bidx < qi, gate, NEG)
        for r in range(MOBA_TOPK):
            mx = jnp.max(gate, axis=0, keepdims=True)
            ix = jnp.min(jnp.where(gate == mx, bidx, nb), axis=0, keepdims=True)
            gate = jnp.where(bidx == ix, -jnp.inf, gate)
            sel_sc[h * MOBA_TOPK + r] = jnp.where(r < qi, ix, -1)
        qb = (qh * (HEAD_DIM ** -0.5)).astype(BF16)
        qb_sc[h] = qb
        s = _dot(kd, qb)
        s = jnp.where(causal, s, NEG)
        m = jnp.max(s, axis=0, keepdims=True)
        p = jnp.exp(s - m)
        m_sc[h] = m
        acc_sc[h] = _dot(vd, p.astype(BF16))

    def body(g, carry):
        kjs, vjs, js = [], [], []
        for u in range(G):
            j = g * G + u
            st = pl.multiple_of(jnp.minimum(j, nb - 1) * blk, blk)
            kjs.append(kb_sc[pl.ds(st, blk), :])
            vjs.append(vt_sc[:, pl.ds(st, blk)])
            js.append(j)
        ss = [[_dot(kjs[u], qb_sc[h]) for u in range(G)] for h in range(2)]
        ps, alphas = [], []
        for h in range(2):
            b = h * MOBA_TOPK
            sm = []
            for u in range(G):
                rowsel = (sel_sc[b] == js[u]) | (sel_sc[b + 1] == js[u]) | (sel_sc[b + 2] == js[u])
                sm.append(jnp.where(rowsel, ss[h][u], NEG))
            m_old = m_sc[h]
            m_new = m_old
            for u in range(G):
                m_new = jnp.maximum(m_new, jnp.max(sm[u], axis=0, keepdims=True))
            alphas.append(jnp.exp(m_old - m_new))
            ps.append(jnp.concatenate([jnp.exp(sm[u] - m_new).astype(BF16) for u in range(G)], axis=0))
            m_sc[h] = m_new
        vcat = jnp.concatenate(vjs, axis=1)
        pv = [_dot(vcat, ps[h]) for h in range(2)]
        for h in range(2):
            acc_sc[h] = alphas[h] * acc_sc[h] + pv[h]
        return carry

    lax.fori_loop(0, (qi + (G - 1)) // G, body, 0)
    a0 = acc_sc[0]
    a1 = acc_sc[1]
    o0 = a0[0:LANES] / a0[LANES:LANES + 1]
    o1 = a1[0:LANES] / a1[LANES:LANES + 1]
    o_ref[...] = jnp.where(head_rows[0], o0, o1).T.astype(o_ref.dtype)


def moba_prompt(q, k, vt, kmean, *, batch, seq):
    tq = MOBA_BLOCK
    nq = seq // tq
    nb = seq // MOBA_BLOCK
    hp = D_GROUP // LANES
    q3 = q.reshape(batch, seq, D_GROUP)
    k3 = k.reshape(batch, seq, D_GROUP)
    km3 = kmean.reshape(batch, nb, D_GROUP)
    nd = LANES + L_ROWS
    out = pl.pallas_call(
        functools.partial(_moba_kernel, tq=tq, seq=seq),
        grid=(batch, hp, nq),
        in_specs=[pl.BlockSpec((None, tq, LANES), lambda b, p, i: (b, i, p)),
                  pl.BlockSpec((None, seq, LANES), lambda b, p, i: (b, 0, p)),
                  pl.BlockSpec((None, LANES, seq), lambda b, p, i: (b, p, 0)),
                  pl.BlockSpec((None, nb, LANES), lambda b, p, i: (b, 0, p))],
        out_specs=pl.BlockSpec((None, tq, LANES), lambda b, p, i: (b, i, p)),
        out_shape=jax.ShapeDtypeStruct((batch, seq, D_GROUP), BF16),
        scratch_shapes=[pltpu.VMEM((nd, seq), BF16),
                        pltpu.VMEM((2, LANES, tq), BF16), pltpu.VMEM((2, 1, tq), F32),
                        pltpu.VMEM((2, nd, tq), F32), pltpu.VMEM((2 * MOBA_TOPK, 1, tq), I32)],
        compiler_params=_cparams(("arbitrary", "arbitrary", "arbitrary"), VMEM_LIMIT),
        name="moba_prompt",
    )(q3, k3, vt, km3)
    return out.reshape(batch * seq, D_GROUP)


DEC_PG = 16
DEC_NEW = 16
DEC_COLS = 32


def _moba_decode_kernel(pt_ref, q2_ref, kn_ref, vn_ref, *rest, n_steps, page, n_q):
    k_refs = rest[:DEC_PG]
    v_refs = rest[DEC_PG:2 * DEC_PG]
    o_ref = rest[2 * DEC_PG]
    s_sc, gate_sc, acc_sc, l_sc, pown_sc = rest[2 * DEC_PG + 1:]
    s = pl.program_id(1)
    ppb = MOBA_BLOCK // page
    nblk = n_steps * DEC_PG // ppb
    nc = DEC_COLS
    scale = HEAD_DIM ** -0.5
    q2 = q2_ref[...]
    lane = lax.broadcasted_iota(I32, (nc, LANES), 1)

    @pl.when(s == 0)
    def _():
        gate_sc[...] = jnp.zeros(gate_sc.shape, F32)

    @pl.when(s < n_steps)
    def _():
        g = gate_sc[...]
        for i in range(DEC_PG):
            pidx = s * DEC_PG + i
            sp = _dot(q2, k_refs[i][...].astype(BF16))
            sf = sp[0:nc] + sp[nc:2 * nc]
            s_sc[:, pl.ds(pl.multiple_of(pidx * page, page), page)] = sf
            g = g + jnp.where(lane == pidx // ppb, jnp.sum(sf, axis=1, keepdims=True), 0.0)
        gate_sc[...] = g

    @pl.when(s == n_steps - 1)
    def _():
        gate = jnp.where(lane < nblk, gate_sc[...] * (1.0 / MOBA_BLOCK), -jnp.inf)
        sels = []
        for r in range(MOBA_TOPK):
            mx = jnp.max(gate, axis=1, keepdims=True)
            ix = jnp.min(jnp.where(gate == mx, lane, LANES), axis=1, keepdims=True)
            gate = jnp.where(lane == ix, -jnp.inf, gate)
            sels.append(ix)
        so = _dot_nt(q2, kn_ref[...].astype(BF16))
        so = (so[0:nc] + so[nc:2 * nc]) * scale
        key_c = lax.broadcasted_iota(I32, so.shape, 1)
        q_r = lax.broadcasted_iota(I32, so.shape, 0) // N_HEADS
        own_ok = key_c <= q_r
        so = jnp.where(own_ok, so, NEG)
        npos = s_sc.shape[1]
        blk_of = lax.broadcasted_iota(I32, (nc, npos), 1) // MOBA_BLOCK
        mask = (blk_of == sels[0]) | (blk_of == sels[1]) | (blk_of == sels[2])
        sc = jnp.where(mask, s_sc[...] * scale, NEG)
        m = jnp.maximum(jnp.max(sc, axis=1, keepdims=True), jnp.max(so, axis=1, keepdims=True))
        p = jnp.where(mask, jnp.exp(sc - m), 0.0)
        po = jnp.where(own_ok, jnp.exp(so - m), 0.0)
        s_sc[...] = p
        pown_sc[...] = po
        l_sc[...] = jnp.sum(p, axis=1, keepdims=True) + jnp.sum(po, axis=1, keepdims=True)
        acc_sc[...] = jnp.zeros(acc_sc.shape, F32)

    @pl.when(s >= n_steps)
    def _():
        acc = acc_sc[...]
        for i in range(DEC_PG):
            pidx = (s - n_steps) * DEC_PG + i
            pj = s_sc[:, pl.ds(pl.multiple_of(pidx * page, page), page)].astype(BF16)
            acc = acc + _dot_nt(pj, v_refs[i][...].astype(BF16))
        acc_sc[...] = acc

    @pl.when(s == 2 * n_steps - 1)
    def _():
        acc = acc_sc[...] + _dot(pown_sc[...].astype(BF16), vn_ref[...].astype(BF16))
        out = acc / l_sc[...]
        hrow = lax.broadcasted_iota(I32, (N_HEADS, D_GROUP), 0)
        hlane = lax.broadcasted_iota(I32, (N_HEADS, D_GROUP), 1) // HEAD_DIM
        rows = []
        for q in range(n_q):
            blk8 = out[q * N_HEADS:(q + 1) * N_HEADS, :]
            rows.append(jnp.sum(jnp.where(hrow == hlane, blk8, 0.0), axis=0, keepdims=True))
        rows.append(jnp.zeros((8 - n_q, D_GROUP), F32))
        o_ref[...] = jnp.concatenate(rows, axis=0)


def moba_decode(page_table, q2, k_new, v_new, cache_kt, cache_vt, *, layer, n_q):
    bsz, n_pages = page_table.shape
    page = cache_kt.shape[3]
    assert n_pages % DEC_PG == 0 and MOBA_BLOCK % page == 0 and page % LANES == 0
    assert n_q <= 8 and n_q * N_HEADS <= DEC_COLS and MOBA_TOPK <= n_pages * page // MOBA_BLOCK <= LANES
    n_steps = n_pages // DEC_PG
    kspec = lambda i: pl.BlockSpec(
        (None, None, D_GROUP, page),
        lambda b, s, pt, i=i: (layer, pt[b, jnp.minimum(s, n_steps - 1) * DEC_PG + i], 0, 0))
    vspec = lambda i: pl.BlockSpec(
        (None, None, D_GROUP, page),
        lambda b, s, pt, i=i: (layer, pt[b, jnp.maximum(s - n_steps, 0) * DEC_PG + i], 0, 0))
    perb = lambda r, w: pl.BlockSpec((None, r, w), lambda b, s, pt: (b, 0, 0))
    return pl.pallas_call(
        functools.partial(_moba_decode_kernel, n_steps=n_steps, page=page, n_q=n_q),
        grid_spec=pltpu.PrefetchScalarGridSpec(
            num_scalar_prefetch=1,
            grid=(bsz, 2 * n_steps),
            in_specs=[perb(2 * DEC_COLS, D_GROUP), perb(DEC_NEW, D_GROUP), perb(DEC_NEW, D_GROUP)]
            + [kspec(i) for i in range(DEC_PG)] + [vspec(i) for i in range(DEC_PG)],
            out_specs=perb(8, D_GROUP),
            scratch_shapes=[pltpu.VMEM((DEC_COLS, n_pages * page), F32), pltpu.VMEM((DEC_COLS, LANES), F32),
                            pltpu.VMEM((DEC_COLS, D_GROUP), F32), pltpu.VMEM((DEC_COLS, 1), F32),
                            pltpu.VMEM((DEC_COLS, DEC_NEW), F32)]),
        out_shape=jax.ShapeDtypeStruct((bsz, 8, D_GROUP), F32),
        compiler_params=_cparams(("arbitrary", "arbitrary"), VMEM_LIMIT),
        name="moba_decode",
    )(page_table, q2, k_new, v_new, *([cache_kt] * DEC_PG), *([cache_vt] * DEC_PG))


def _log_sigmoid(x):
    return jnp.minimum(x, 0.0) - jnp.log1p(jnp.exp(-jnp.abs(x)))


def _mlstm_kernel(qk_ref, v_ref, om_ref, g_ref, cw_ref, cb_ref, bg_ref, gh_ref,
                  conv0_ref, c0_ref, n0_ref, m0_ref,
                  h_ref, cout_ref, nout_ref, mout_ref,
                  conv_sc, c_sc, n_sc, m_sc, *, L, t_valid):
    ci = pl.program_id(1)
    nh, dh, dg = N_HEADS, HEAD_DIM, D_GROUP

    @pl.when(ci == 0)
    def _():
        conv_sc[...] = conv0_ref[...]
        c_sc[...] = c0_ref[...]
        n_sc[...] = n0_ref[...]
        m_sc[...] = m0_ref[...]

    x = qk_ref[...]
    xc = jnp.concatenate([conv_sc[...], x], axis=0)
    y = cb_ref[...]
    for j in range(CONV_W):
        off = 8 - (CONV_W - 1) + j
        y = y + xc[off:off + L] * cw_ref[j:j + 1, :]
    conv_sc[...] = x[L - 8:L]
    qk = y * jax.nn.sigmoid(y)
    q_all = qk[:, :dg].astype(BF16)
    k_all = qk[:, dg:] * (dh ** -0.5)
    v_all = v_ref[...]
    om = om_ref[...]

    g = g_ref[...] + bg_ref[...]
    lf = _log_sigmoid(g)
    ri = lax.broadcasted_iota(I32, (L, L), 0)
    cj = lax.broadcasted_iota(I32, (L, L), 1)
    causal = cj <= ri
    tril = jnp.where(causal, 1.0, 0.0).astype(BF16)
    lf3 = _split3(lf)
    b_col = _dot(tril, lf3[0]) + _dot(tril, lf3[1]) + _dot(tril, lf3[2])
    sel_r = lax.broadcasted_iota(I32, (8, LANES), 0)
    sel_c = lax.broadcasted_iota(I32, (8, LANES), 1)
    sel_i = jnp.where(sel_c == sel_r, 1.0, 0.0).astype(BF16)
    sel_f = jnp.where(sel_c == sel_r + nh, 1.0, 0.0).astype(BF16)

    def rows_of(sel, a):
        a3 = _split3(a)
        return _dot_nt(sel, a3[0]) + _dot_nt(sel, a3[1]) + _dot_nt(sel, a3[2])

    li_row = rows_of(sel_i, g)
    b_row = rows_of(sel_f, b_col)
    rvalid = lax.broadcasted_iota(I32, (L, 1), 0) < t_valid
    heads = range(nh)
    hsl = [slice(h * dh, (h + 1) * dh) for h in heads]
    qs = [q_all[:, hsl[h]] for h in heads]
    ks = [k_all[:, hsl[h]] for h in heads]
    vs = [v_all[:, hsl[h]] for h in heads]
    cs = [c_sc[h] for h in heads]
    ns = [n_sc[h:h + 1, :] for h in heads]
    qk_raw = [_dot_nt(qs[h], ks[h].astype(BF16)) for h in heads]
    qc = [_dot(qs[h], cs[h].astype(BF16)) for h in heads]
    bcs, m_prevs, m_ts, w_inters, ss = [], [], [], [], []
    for h in heads:
        bc = b_col[:, nh + h:nh + h + 1]
        log_d = jnp.where(causal, bc - b_row[h:h + 1, :] + li_row[h:h + 1, :], -jnp.inf)
        m_prev = m_sc[h:h + 1, 0:1]
        log_inter = m_prev + bc
        m_t = jnp.maximum(log_inter, jnp.max(log_d, axis=-1, keepdims=True))
        dw = jnp.exp(log_d - m_t)
        bcs.append(bc)
        m_prevs.append(m_prev)
        m_ts.append(m_t)
        w_inters.append(jnp.exp(log_inter - m_t))
        ss.append(qk_raw[h] * dw)
    sv = [_dot(ss[h].astype(BF16), vs[h]) for h in heads]
    kws, decays, m_news = [], [], []
    for h in heads:
        m_new = m_ts[h][t_valid - 1:t_valid, :]
        b_last = bcs[h][t_valid - 1:t_valid, :]
        w_state = jnp.where(rvalid, jnp.exp(b_last - bcs[h] + g[:, h:h + 1] - m_new), 0.0)
        decays.append(jnp.exp(m_prevs[h] + b_last - m_new))
        kws.append(ks[h] * w_state)
        m_news.append(m_new)
    kv = [lax.dot_general(kws[h].astype(BF16), vs[h], _TN, preferred_element_type=F32) for h in heads]
    outs = []
    for h in heads:
        num = sv[h] + w_inters[h] * qc[h]
        qn = jnp.sum(qs[h].astype(F32) * ns[h], axis=-1, keepdims=True)
        den = jnp.sum(ss[h], axis=-1, keepdims=True) + w_inters[h] * qn
        hh = num / jnp.maximum(jnp.abs(den), jnp.exp(-m_ts[h]))
        c_sc[h] = decays[h] * cs[h] + kv[h]
        n_sc[h:h + 1, :] = decays[h] * ns[h] + jnp.sum(kws[h], axis=0, keepdims=True)
        m_sc[h:h + 1, :] = jnp.broadcast_to(m_news[h], (1, LANES))
        hn = hh * lax.rsqrt(jnp.mean(hh * hh, axis=-1, keepdims=True) + EPS) * gh_ref[:, hsl[h]]
        outs.append(hn * jax.nn.sigmoid(om[:, hsl[h]]))
    h_ref[...] = jnp.concatenate(outs, axis=-1).astype(h_ref.dtype)
    cout_ref[...] = c_sc[...]
    nout_ref[...] = n_sc[...]
    mout_ref[...] = m_sc[...]


def mlstm(qk, vm, om, gates, conv_w, conv_b, bg_pad, g_head, conv0, c0, n0, m0b, *, L, t_valid):
    b, t, _ = qk.shape
    nh, dh, dg = N_HEADS, HEAD_DIM, D_GROUP
    nc = t // L
    tok = lambda w: pl.BlockSpec((None, L, w), lambda i, c: (i, c, 0))
    full2 = lambda r, w: pl.BlockSpec((r, w), lambda i, c: (0, 0))
    return pl.pallas_call(
        functools.partial(_mlstm_kernel, L=L, t_valid=t_valid),
        grid=(b, nc),
        in_specs=[tok(2 * dg), tok(dg), tok(dg), tok(GATE_PAD),
                  full2(CONV_W, 2 * dg), full2(1, 2 * dg), full2(1, GATE_PAD), full2(1, dg),
                  pl.BlockSpec((None, 8, 2 * dg), lambda i, c: (i, 0, 0)),
                  pl.BlockSpec((None, nh, dh, dh), lambda i, c: (i, 0, 0, 0)),
                  pl.BlockSpec((None, nh, dh), lambda i, c: (i, 0, 0)),
                  pl.BlockSpec((None, nh, LANES), lambda i, c: (i, 0, 0))],
        out_specs=[tok(dg),
                   pl.BlockSpec((None, nh, dh, dh), lambda i, c: (i, 0, 0, 0)),
                   pl.BlockSpec((None, nh, dh), lambda i, c: (i, 0, 0)),
                   pl.BlockSpec((None, nh, LANES), lambda i, c: (i, 0, 0))],
        out_shape=[jax.ShapeDtypeStruct((b, t, dg), BF16),
                   jax.ShapeDtypeStruct((b, nh, dh, dh), F32),
                   jax.ShapeDtypeStruct((b, nh, dh), F32),
                   jax.ShapeDtypeStruct((b, nh, LANES), F32)],
        scratch_shapes=[pltpu.VMEM((8, 2 * dg), F32), pltpu.VMEM((nh, dh, dh), F32),
                        pltpu.VMEM((nh, dh), F32), pltpu.VMEM((nh, LANES), F32)],
        compiler_params=_cparams(("arbitrary", "arbitrary"), VMEM_LIMIT),
        name="mlstm",
    )(qk, vm, om, gates, conv_w, conv_b, bg_pad, g_head, conv0, c0, n0, m0b)


def _outproj_kernel(att_ref, hm_ref, x_ref, gt_ref, sc_ref, sh_ref, g_ref, wo_ref, wrh_ref, wrl_ref, br_ref,
                    *rest):
    xo_ref, h2_ref, ti_ref, tw_ref = rest[-4:]
    dg = D_GROUP
    mix = _dot(att_ref[...], wo_ref[0:dg, :]) + _dot(hm_ref[...], wo_ref[dg:2 * dg, :])
    xn = x_ref[...] + gt_ref[...] * mix
    xo_ref[...] = xn
    ms = jnp.mean(xn * xn, axis=-1, keepdims=True)
    y = xn * lax.rsqrt(ms + EPS) * g_ref[...]
    h2 = y * (1.0 + sc_ref[...]) + sh_ref[...]
    hh, hl = _split2(h2)
    h2_ref[...] = hh.astype(F32)
    wrh = wrh_ref[...]
    logits = _dot(hh, wrh) + _dot(hh, wrl_ref[...]) + _dot(hl, wrh) + br_ref[...]
    col = lax.broadcasted_iota(I32, logits.shape, 1)
    lg = jnp.where(col < N_EXPERTS, logits, -jnp.inf)
    vals, idxs = [], []
    for _ in range(TOP_K):
        mx = jnp.max(lg, axis=-1, keepdims=True)
        ix = jnp.min(jnp.where(lg == mx, col, LANES), axis=-1, keepdims=True)
        lg = jnp.where(col == ix, -jnp.inf, lg)
        vals.append(mx)
        idxs.append(ix)
    es = [jnp.exp(v - vals[0]) for v in vals]
    den = es[0] + es[1] + es[2] + es[3]
    ti = jnp.zeros(logits.shape, I32)
    tw = jnp.zeros(logits.shape, F32)
    for r in range(TOP_K):
        ti = jnp.where(col == r, idxs[r], ti)
        tw = jnp.where(col == r, es[r] / den, tw)
    ti_ref[...] = ti
    tw_ref[...] = tw


def out_proj(att, hm, x, mod, g_ffn, w_out, wr_hi, wr_lo, br_pad, routed, *, n_routed, row0, tm, tiles_per_group):
    n, d = x.shape
    rm = mod.shape[1]
    dg = D_GROUP
    b0 = row0 // tm
    row = lambda i: (i, 0)
    rrow = lambda i: (b0 + i, 0)
    modspec = lambda c: pl.BlockSpec((None, rm, d), lambda i, c=c: (i // tiles_per_group, 0, c))
    const = lambda r, w: pl.BlockSpec((r, w), lambda i: (0, 0))
    in_specs = [pl.BlockSpec((tm, dg), row), pl.BlockSpec((tm, dg), row), pl.BlockSpec((tm, d), row),
                modspec(2), modspec(4), modspec(3),
                const(1, d), const(2 * dg, d), const(d, LANES), const(d, LANES), const(1, LANES)]
    args = [att, hm, x, mod, mod, mod, g_ffn, w_out, wr_hi, wr_lo, br_pad]
    aliases = {}
    if routed is not None:
        aliases = {len(args) + j: 1 + j for j in range(3)}
        in_specs += [pl.BlockSpec(memory_space=pl.ANY)] * 3
        args += list(routed)
    return pl.pallas_call(
        _outproj_kernel,
        grid=(n // tm,),
        in_specs=in_specs,
        out_specs=[pl.BlockSpec((tm, d), row), pl.BlockSpec((tm, d), rrow),
                   pl.BlockSpec((tm, LANES), rrow), pl.BlockSpec((tm, LANES), rrow)],
        out_shape=[jax.ShapeDtypeStruct((n, d), F32), jax.ShapeDtypeStruct((n_routed, d), F32),
                   jax.ShapeDtypeStruct((n_routed, LANES), I32), jax.ShapeDtypeStruct((n_routed, LANES), F32)],
        input_output_aliases=aliases,
        compiler_params=_cparams(("arbitrary",), VMEM_LIMIT),
        name="out_proj",
    )(*args)


def _rank_kernel(ti_ref, rank_ref, cnt_ref, base_sc):
    i = pl.program_id(0)

    @pl.when(i == 0)
    def _():
        base_sc[...] = jnp.zeros(base_sc.shape, F32)

    ti = ti_ref[...]
    tm = ti.shape[0]
    col = lax.broadcasted_iota(I32, ti.shape, 1)
    es = []
    oh = jnp.zeros(ti.shape, F32)
    for k in range(TOP_K):
        ek = jnp.sum(jnp.where(col == k, ti, 0), axis=-1, keepdims=True)
        es.append(ek)
        oh = oh + jnp.where(col == ek, 1.0, 0.0)
    ri = lax.broadcasted_iota(I32, (tm, tm), 0)
    ci = lax.broadcasted_iota(I32, (tm, tm), 1)
    tri = jnp.where(ci < ri, 1.0, 0.0).astype(BF16)
    rank = _dot(tri, oh.astype(BF16)) + base_sc[0:1, :]
    out = jnp.zeros(ti.shape, F32)
    for k in range(TOP_K):
        rk = jnp.sum(jnp.where(col == es[k], rank, 0.0), axis=-1, keepdims=True)
        out = jnp.where(col == k, rk, out)
    rank_ref[...] = out.astype(I32)
    tot = base_sc[0:1, :] + jnp.sum(oh, axis=0, keepdims=True)
    base_sc[...] = jnp.broadcast_to(tot, base_sc.shape)
    cnt_ref[...] = base_sc[...].astype(I32)


def expert_ranks(ti, *, tm):
    n = ti.shape[0]
    return pl.pallas_call(
        _rank_kernel,
        grid=(n // tm,),
        in_specs=[pl.BlockSpec((tm, LANES), lambda i: (i, 0))],
        out_specs=[pl.BlockSpec((tm, LANES), lambda i: (i, 0)), pl.BlockSpec((8, LANES), lambda i: (0, 0))],
        out_shape=[jax.ShapeDtypeStruct((n, LANES), I32), jax.ShapeDtypeStruct((8, LANES), I32)],
        scratch_shapes=[pltpu.VMEM((8, LANES), F32)],
        compiler_params=_cparams(("arbitrary",)),
        name="expert_ranks",
    )(ti)


def _moe_kernel(be_ref, first_ref, nused_ref, x_ref, wgu_ref, bgu_ref, wd_ref, bd_ref, y_ref, wgu_sc, wd_sc):
    i = pl.program_id(0)
    f = wd_ref.shape[0]

    @pl.when(first_ref[i] == 1)
    def _():
        wgu_sc[...] = wgu_ref[...].astype(BF16)
        wd_sc[...] = wd_ref[...].astype(BF16)

    @pl.when(i < nused_ref[0])
    def _():
        gu = _dot(x_ref[...].astype(BF16), wgu_sc[...]) + bgu_ref[...]
        gg = jnp.minimum(gu[:, :f], SWIGLU_LIMIT)
        uu = jnp.clip(gu[:, f:], -SWIGLU_LIMIT, SWIGLU_LIMIT)
        act = (uu + 1.0) * gg * jax.nn.sigmoid(gg * SWIGLU_ALPHA)
        y_ref[...] = _dot(act.astype(BF16), wd_sc[...]) + bd_ref[...]


def moe_mlp(blk_exp, blk_first, n_used, xg, w_gu, b_gu, w_down, b_down, *, layer, bm):
    m, d = xg.shape
    depth, e, _, f2 = w_gu.shape
    f = f2 // 2
    nblk = m // bm
    xmap = lambda i, be, fi, nu: (jnp.minimum(i, nu[0] - 1), 0)
    wmap = lambda i, be, fi, nu: (layer, be[i], 0, 0)
    return pl.pallas_call(
        _moe_kernel,
        grid_spec=pltpu.PrefetchScalarGridSpec(
            num_scalar_prefetch=3,
            grid=(nblk,),
            in_specs=[pl.BlockSpec((bm, d), xmap),
                      pl.BlockSpec((None, None, d, f2), wmap),
                      pl.BlockSpec((None, None, 1, f2), wmap),
                      pl.BlockSpec((None, None, f, d), wmap),
                      pl.BlockSpec((None, None, 1, d), wmap)],
            out_specs=pl.BlockSpec((bm, d), xmap),
            scratch_shapes=[pltpu.VMEM((d, f2), BF16), pltpu.VMEM((f, d), BF16)]),
        out_shape=jax.ShapeDtypeStruct((m, d), F32),
        compiler_params=_cparams(("arbitrary",), VMEM_LIMIT),
        name="moe_mlp",
    )(blk_exp, blk_first, n_used, xg, w_gu, b_gu.reshape(depth, e, 1, f2), w_down, b_down.reshape(depth, e, 1, d))


def _combine_kernel(y_ref, tw_ref, x_ref, gt_ref, o_ref):
    tw = tw_ref[...]
    ff = y_ref[0] * tw[:, 0:1]
    for k in range(1, TOP_K):
        ff = ff + y_ref[k] * tw[:, k:k + 1]
    o_ref[...] = x_ref[...] + gt_ref[...] * ff


def moe_combine(yg, tw, x, mod, *, row0, tm, tiles_per_group):
    n, d = x.shape
    rm = mod.shape[1]
    b0 = row0 // tm
    return pl.pallas_call(
        _combine_kernel,
        grid=(n // tm,),
        in_specs=[pl.BlockSpec((TOP_K, tm, d), lambda i: (0, b0 + i, 0)),
                  pl.BlockSpec((tm, LANES), lambda i: (b0 + i, 0)),
                  pl.BlockSpec((tm, d), lambda i: (i, 0)),
                  pl.BlockSpec((None, rm, d), lambda i: (i // tiles_per_group, 0, N_ADA - 1))],
        out_specs=pl.BlockSpec((tm, d), lambda i: (i, 0)),
        out_shape=jax.ShapeDtypeStruct((n, d), F32),
        compiler_params=_cparams(("arbitrary",), VMEM_LIMIT),
        name="moe_combine",
    )(yg, tw, x, mod)


def _rms_kernel(x_ref, g_ref, o_ref):
    x = x_ref[...]
    o_ref[...] = x * lax.rsqrt(jnp.mean(x * x, axis=-1, keepdims=True) + EPS) * g_ref[...]


def final_norm(x, g, *, tm):
    n, d = x.shape
    return pl.pallas_call(
        _rms_kernel,
        grid=(n // tm,),
        in_specs=[pl.BlockSpec((tm, d), lambda i: (i, 0)), pl.BlockSpec((1, d), lambda i: (0, 0))],
        out_specs=pl.BlockSpec((tm, d), lambda i: (i, 0)),
        out_shape=jax.ShapeDtypeStruct((n, d), F32),
        compiler_params=_cparams(("arbitrary",)),
        name="final_norm",
    )(x, g)


def _decode_q2(q):
    b, ts, _ = q.shape
    q4 = q.reshape(b, ts, N_HEADS, HEAD_DIM)
    eye = jnp.eye(N_HEADS, dtype=F32)
    qbd = q4[:, :, None, :, :] * eye[None, None, :, :, None]
    qbd = qbd.reshape(b, ts * N_HEADS, D_GROUP)
    qbd = jnp.pad(qbd, ((0, 0), (0, DEC_COLS - ts * N_HEADS), (0, 0)))
    hi = qbd.astype(BF16)
    lo = (qbd - hi.astype(F32)).astype(BF16)
    return jnp.concatenate([hi, lo], axis=1)


def _route(ti, bm):
    n = ti.shape[0]
    na = n * TOP_K
    tm = max(t for t in range(8, 513, 8) if n % t == 0)
    rank, cnt = expert_ranks(ti, tm=tm)
    counts = cnt[0, :N_EXPERTS]
    padded = ((counts + bm - 1) // bm) * bm
    pad_end = jnp.cumsum(padded)
    pad_start = pad_end - padded
    e4 = ti[:, :TOP_K]
    onehot = e4[:, :, None] == jnp.arange(N_EXPERTS, dtype=I32)[None, None, :]
    pos = jnp.sum(jnp.where(onehot, pad_start[None, None, :], 0), axis=-1) + rank[:, :TOP_K]
    nblk = -(-(na + N_EXPERTS * (bm - 1)) // bm)
    m = nblk * bm
    tok = jnp.broadcast_to(jnp.arange(n, dtype=I32)[:, None], (n, TOP_K))
    row_tok = jnp.zeros((m,), I32).at[pos.reshape(-1)].set(tok.reshape(-1), unique_indices=True)
    blk_start = jnp.arange(nblk, dtype=I32) * bm
    blk_exp = jnp.minimum(jnp.sum(blk_start[:, None] >= pad_end[None, :], axis=1), N_EXPERTS - 1).astype(I32)
    n_used = (pad_end[-1] // bm).astype(I32)
    blk_first = jnp.concatenate([jnp.ones((1,), I32), (blk_exp[1:] != blk_exp[:-1]).astype(I32)])
    blk_first = jnp.where(jnp.arange(nblk) < n_used, blk_first, 0).astype(I32)
    return row_tok, blk_exp, blk_first, n_used.reshape(1), pos


def _layer(l, depth, xp, xs, modp, mods, P, S, W, stacks):
    dg = D_GROUP
    Bp, T = P["batch"], P["seq"]
    Bs, Ts = S["batch"], S["seq"]
    d = xp.shape[1]
    tm = 512
    kst_p, vst_p = stacks
    qa, kst_p, vst_p, kb, vbt, qk, vm, om, gt, kmean = in_proj(
        xp, modp, W["g_mix"][l], W["w_in"][l], kst_p, vst_p, layer=l, depth=depth,
        tm=tm, tiles_per_group=T // tm, fresh=True)
    att_p = moba_prompt(qa, kb, vbt, kmean.reshape(-1, dg), batch=Bp, seq=T)
    Lp = 256
    hm_p, c_p, n_p, m_p = mlstm(
        qk.reshape(Bp, T, 2 * dg), vm.reshape(Bp, T, dg), om.reshape(Bp, T, dg), gt.reshape(Bp, T, GATE_PAD),
        W["conv_w"][l], W["conv_b"][l], W["bg_pad"][l], W["g_head"][l],
        jnp.zeros((Bp, 8, 2 * dg), F32), jnp.zeros((Bp, N_HEADS, HEAD_DIM, HEAD_DIM), F32),
        jnp.zeros((Bp, N_HEADS, HEAD_DIM), F32), jnp.zeros((Bp, N_HEADS, LANES), F32), L=Lp, t_valid=Lp)
    conv_p = qk.reshape(Bp, T, 2 * dg)[:, T - (CONV_W - 1):]
    np_ = Bp * T
    ntot = np_ + Bs * Ts
    xp1, *routed = out_proj(att_p, hm_p.reshape(np_, dg), xp, modp, W["g_ffn"][l], W["w_out"][l],
                            W["wr_hi"][l], W["wr_lo"][l], W["br_pad"][l], None,
                            n_routed=ntot, row0=0, tm=tm, tiles_per_group=T // tm)
    ns = Bs * Ts
    qa_s, ka_s, va_s, _, _, qk_s, vm_s, om_s, gt_s = in_proj(
        xs, mods, W["g_mix"][l], W["w_in"][l], None, None, layer=l, depth=depth,
        tm=ns, tiles_per_group=1, fresh=False)
    newpad = lambda a: jnp.pad(a.reshape(Bs, Ts, dg), ((0, 0), (0, DEC_NEW - Ts), (0, 0)))
    att_s = moba_decode(S["page_table"], _decode_q2(qa_s.reshape(Bs, Ts, dg)), newpad(ka_s), newpad(va_s),
                        S["cache_kt"], S["cache_vt"], layer=l, n_q=Ts)
    att_s = att_s[:, :Ts].reshape(ns, dg).astype(BF16)
    Ls = 16
    padt = lambda a: jnp.pad(a.reshape(Bs, Ts, -1), ((0, 0), (0, Ls - Ts), (0, 0)))
    conv0 = jnp.pad(S["state_conv"][l], ((0, 0), (8 - (CONV_W - 1), 0), (0, 0)))
    m0b = jnp.broadcast_to(S["state_m"][l][:, :, None], (Bs, N_HEADS, LANES))
    hm_s, c_s, n_s, m_s = mlstm(padt(qk_s), padt(vm_s), padt(om_s), padt(gt_s),
                                W["conv_w"][l], W["conv_b"][l], W["bg_pad"][l], W["g_head"][l],
                                conv0, S["state_c"][l], S["state_n"][l], m0b, L=Ls, t_valid=Ts)
    hm_s = hm_s[:, :Ts].reshape(ns, dg)
    conv_s = qk_s.reshape(Bs, Ts, 2 * dg)[:, Ts - (CONV_W - 1):]
    xs1, h2, ti, tw = out_proj(att_s, hm_s, xs, mods, W["g_ffn"][l], W["w_out"][l],
                               W["wr_hi"][l], W["wr_lo"][l], W["br_pad"][l], routed,
                               n_routed=ntot, row0=np_, tm=ns, tiles_per_group=1)
    row_tok, blk_exp, blk_first, n_used, pos = _route(ti, MOE_BM)
    xg = h2[row_tok]
    yb = moe_mlp(blk_exp, blk_first, n_used, xg, W["w_gu"], W["b_gu"], W["w_down"], W["b_down"], layer=l, bm=MOE_BM)
    yg = yb[pos.T.reshape(-1)].reshape(TOP_K, ntot, d)
    xp2 = moe_combine(yg, tw, xp1, modp, row0=0, tm=tm, tiles_per_group=T // tm)
    xs2 = moe_combine(yg, tw, xs1, mods, row0=np_, tm=ns, tiles_per_group=1)
    outs_p = (c_p, n_p, m_p[:, :, 0], conv_p)
    outs_s = (ka_s.reshape(Bs, Ts, N_HEADS, HEAD_DIM), va_s.reshape(Bs, Ts, N_HEADS, HEAD_DIM),
              c_s, n_s, m_s[:, :, 0], conv_s)
    return xp2, xs2, outs_p, outs_s, (kst_p, vst_p)


def kernel(x_prompt, x_sample, c_prompt, c_sample, cache_k, cache_v, page_table, state_c, state_n, state_m,
           state_conv, w_ada, b_ada, g_mix, w_in, b_gate, conv_w, conv_b, g_head, w_out, g_ffn, w_router,
           b_router, w_gu, b_gu, w_down, b_down, g_final):
    depth = w_ada.shape[0]
    Bp, T, d = x_prompt.shape
    Bs, Ts, _ = x_sample.shape
    dg = D_GROUP
    n_gate = 2 * N_HEADS
    w_in_pad = jnp.pad(w_in, ((0, 0), (0, 0), (0, GATE_PAD - n_gate))).astype(BF16)
    wr_pad = jnp.pad(w_router, ((0, 0), (0, 0), (0, LANES - N_EXPERTS)))
    wr_hi = wr_pad.astype(BF16)
    wr_lo = (wr_pad - wr_hi.astype(F32)).astype(BF16)
    W = dict(
        g_mix=g_mix.reshape(depth, 1, d), w_in=w_in_pad,
        conv_w=conv_w, conv_b=conv_b.reshape(depth, 1, 2 * dg),
        bg_pad=jnp.pad(b_gate, ((0, 0), (0, GATE_PAD - n_gate))).reshape(depth, 1, GATE_PAD),
        g_head=g_head.reshape(depth, 1, dg), w_out=w_out.astype(BF16), g_ffn=g_ffn.reshape(depth, 1, d),
        wr_hi=wr_hi, wr_lo=wr_lo,
        br_pad=jnp.pad(b_router, ((0, 0), (0, LANES - N_EXPERTS))).reshape(depth, 1, LANES),
        w_gu=w_gu, b_gu=b_gu, w_down=w_down, b_down=b_down)
    n_pool, page = cache_k.shape[1], cache_k.shape[2]
    assert (page_table.shape[1] * page) % MOBA_BLOCK == 0
    pages_t = lambda c: c.transpose(0, 1, 3, 4, 2).reshape(depth, n_pool, dg, page)
    S = dict(batch=Bs, seq=Ts, page_table=page_table, cache_kt=pages_t(cache_k), cache_vt=pages_t(cache_v),
             state_c=state_c, state_n=state_n, state_m=state_m, state_conv=state_conv)
    P = dict(batch=Bp, seq=T)
    nc = Bp + Bs
    rpad = -(-nc // 8) * 8
    c_all = jnp.pad(jnp.concatenate([c_prompt, c_sample], axis=0), ((0, rpad - nc), (0, 0)))
    mod = ada_mod(c_all, w_ada, b_ada)
    xp = x_prompt.reshape(Bp * T, d)
    xs = x_sample.reshape(Bs * Ts, d)
    acc_p, acc_s = [], []
    stacks = (None, None)
    for l in range(depth):
        modp = mod[l, :Bp].reshape(Bp, 1, N_ADA * d)
        mods = jnp.repeat(mod[l, Bp:nc], Ts, axis=0).reshape(1, Bs * Ts, N_ADA * d)
        xp, xs, op, os_, stacks = _layer(l, depth, xp, xs, modp, mods, P, S, W, stacks)
        acc_p.append(op)
        acc_s.append(os_)
    y_p = final_norm(xp, g_final.reshape(1, d), tm=512).reshape(Bp, T, d)
    y_s = final_norm(xs, g_final.reshape(1, d), tm=Bs * Ts).reshape(Bs, Ts, d)
    stack = lambda acc, i: jnp.stack([a[i] for a in acc])
    rows = lambda a: a.reshape(depth, Bp, N_HEADS, HEAD_DIM, T).transpose(0, 1, 4, 2, 3)
    return ((y_p, y_s, rows(stacks[0]), rows(stacks[1])) + tuple(stack(acc_p, i) for i in range(4))
            + tuple(stack(acc_s, i) for i in range(6)))
```

```python
import functools

import jax
import jax.numpy as jnp
from jax import lax
from jax.experimental import pallas as pl
from jax.experimental.pallas import tpu as pltpu

F32 = jnp.float32
BF16 = jnp.bfloat16
I32 = jnp.int32
U32 = jnp.uint32

HEAD_DIM = 64
N_HEADS = 8
D_GROUP = N_HEADS * HEAD_DIM
MOBA_BLOCK = 256
MOBA_TOPK = 3
CONV_W = 4
N_EXPERTS = 32
TOP_K = 4
SWIGLU_ALPHA = 1.702
SWIGLU_LIMIT = 7.0
N_ADA = 6
EPS = 1e-6
NEG = -1e30
LANES = 128
GATE_PAD = LANES
MOE_BM = 512
VMEM_LIMIT = 56 * 1024 * 1024


def _cparams(sem, vmem=None):
    return pltpu.CompilerParams(dimension_semantics=sem, vmem_limit_bytes=vmem)


def _split2(x):
    hi = x.astype(BF16)
    lo = (x - hi.astype(F32)).astype(BF16)
    return hi, lo


def _split3(x):
    hi = x.astype(BF16)
    r = x - hi.astype(F32)
    mid = r.astype(BF16)
    lo = (r - mid.astype(F32)).astype(BF16)
    return hi, mid, lo


_NT = (((1,), (1,)), ((), ()))
_TN = (((0,), (0,)), ((), ()))


def _dot(a, b):
    return jnp.dot(a, b, preferred_element_type=F32)


def _dot_nt(a, b):
    return lax.dot_general(a, b, _NT, preferred_element_type=F32)


def _ada_kernel(c_ref, w_ref, b_ref, o_ref):
    c = c_ref[...]
    s = (c * jax.nn.sigmoid(c)).astype(BF16)
    o_ref[...] = _dot(s, w_ref[...].astype(BF16)) + b_ref[...]


def ada_mod(c_all, w_ada, b_ada):
    depth, d, nd = w_ada.shape
    r = c_all.shape[0]
    nj = nd // d
    return pl.pallas_call(
        _ada_kernel,
        grid=(depth, nj),
        in_specs=[pl.BlockSpec((r, d), lambda l, j: (0, 0)),
                  pl.BlockSpec((None, d, d), lambda l, j: (l, 0, j)),
                  pl.BlockSpec((None, 1, d), lambda l, j: (l, 0, j))],
        out_specs=pl.BlockSpec((None, r, d), lambda l, j: (l, 0, j)),
        out_shape=jax.ShapeDtypeStruct((depth, r, nd), F32),
        compiler_params=_cparams(("arbitrary", "arbitrary")),
        name="ada_mod",
    )(c_all, w_ada, b_ada.reshape(depth, 1, nd))


def _inproj_kernel(x_ref, g_ref, sc_ref, sh_ref, w_ref, *rest, fresh, aliased):
    if aliased:
        rest = rest[2:]
    qa_ref, ks_ref, vs_ref, kb_ref, vb_ref, qk_ref, vm_ref, om_ref, gt_ref = rest[:9]
    x = x_ref[...]
    ms = jnp.mean(x * x, axis=-1, keepdims=True)
    y = x * lax.rsqrt(ms + EPS) * g_ref[...]
    h = (y * (1.0 + sc_ref[...]) + sh_ref[...]).astype(BF16)
    dg = D_GROUP
    tm = x.shape[0]

    def seg(a, b):
        return _dot(h, w_ref[:, a:b])

    qa_ref[...] = seg(0, dg)
    ka = seg(dg, 2 * dg)
    va = seg(2 * dg, 3 * dg)
    kb_ref[...] = ka.astype(BF16)
    if fresh:
        ks_ref[...] = ka.T
        vt = va.T
        vs_ref[...] = vt
        vb_ref[...] = vt.astype(BF16)
        km_ref = rest[9]
        nblk = tm // MOBA_BLOCK
        rows = [jnp.sum(ka[i * MOBA_BLOCK:(i + 1) * MOBA_BLOCK], axis=0, keepdims=True) for i in range(nblk)]
        km_ref[...] = jnp.concatenate(rows, axis=0) * (1.0 / MOBA_BLOCK)
    else:
        ks_ref[...] = ka
        vs_ref[...] = va
        vb_ref[...] = va.astype(BF16)
    qk_ref[...] = seg(3 * dg, 5 * dg)
    vm_ref[...] = seg(5 * dg, 6 * dg).astype(BF16)
    om_ref[...] = seg(6 * dg, 7 * dg)
    gt_ref[...] = seg(7 * dg, 7 * dg + GATE_PAD)


def in_proj(x, mod, g, w_pad, kstack, vstack, *, layer, depth, tm, tiles_per_group, fresh):
    n, d = x.shape
    rm = mod.shape[1]
    nw = w_pad.shape[1]
    dg = D_GROUP
    grid = (n // tm,)
    row = lambda i: (i, 0)
    modspec = lambda c: pl.BlockSpec((None, rm, d), lambda i, c=c: (i // tiles_per_group, 0, c))
    if fresh:
        ngrp = n // (tm * tiles_per_group)
        seq = tm * tiles_per_group
        tmap = lambda i: (i // tiles_per_group, 0, i % tiles_per_group)
        kv_shape = jax.ShapeDtypeStruct((depth, ngrp, dg, seq), F32)
        kv_spec = pl.BlockSpec((None, None, dg, tm), lambda i: (layer,) + tmap(i))
        vb_shape = jax.ShapeDtypeStruct((ngrp, dg, seq), BF16)
        vb_spec = pl.BlockSpec((None, dg, tm), tmap)
    else:
        kv_shape = jax.ShapeDtypeStruct((n, dg), F32)
        kv_spec = pl.BlockSpec((tm, dg), row)
        vb_shape = jax.ShapeDtypeStruct((n, dg), BF16)
        vb_spec = pl.BlockSpec((tm, dg), row)
    out_shapes = [jax.ShapeDtypeStruct((n, dg), F32), kv_shape, kv_shape,
                  jax.ShapeDtypeStruct((n, dg), BF16), vb_shape,
                  jax.ShapeDtypeStruct((n, 2 * dg), F32), jax.ShapeDtypeStruct((n, dg), BF16),
                  jax.ShapeDtypeStruct((n, dg), F32), jax.ShapeDtypeStruct((n, GATE_PAD), F32)]
    out_specs = [pl.BlockSpec((tm, dg), row), kv_spec, kv_spec,
                 pl.BlockSpec((tm, dg), row), vb_spec,
                 pl.BlockSpec((tm, 2 * dg), row), pl.BlockSpec((tm, dg), row),
                 pl.BlockSpec((tm, dg), row), pl.BlockSpec((tm, GATE_PAD), row)]
    if fresh:
        nblk = tm // MOBA_BLOCK
        out_shapes.append(jax.ShapeDtypeStruct((n // tm, nblk, dg), F32))
        out_specs.append(pl.BlockSpec((None, nblk, dg), lambda i: (i, 0, 0)))
    in_specs = [pl.BlockSpec((tm, d), row),
                pl.BlockSpec((1, d), lambda i: (0, 0)),
                modspec(1), modspec(0),
                pl.BlockSpec((d, nw), lambda i: (0, 0))]
    args = [x, g, mod, mod, w_pad]
    aliased = kstack is not None
    aliases = {}
    if aliased:
        in_specs += [pl.BlockSpec(memory_space=pl.ANY)] * 2
        args += [kstack, vstack]
        aliases = {5: 1, 6: 2}
    return pl.pallas_call(
        functools.partial(_inproj_kernel, fresh=fresh, aliased=aliased),
        grid=grid,
        in_specs=in_specs,
        out_specs=out_specs,
        out_shape=out_shapes,
        input_output_aliases=aliases,
        compiler_params=_cparams(("arbitrary",), VMEM_LIMIT),
        name="in_proj",
    )(*args)


L_ROWS = 16
MOBA_G = 4


def _moba_kernel(q_ref, k_ref, v_ref, km_ref, o_ref, vt_sc, qb_sc, m_sc, acc_sc, sel_sc, *, tq, seq):
    qi = pl.program_id(2)
    nb = km_ref.shape[0]
    blk = MOBA_BLOCK
    nd = LANES + L_ROWS
    G = MOBA_G
    kb_sc = k_ref

    @pl.when(qi == 0)
    def _():
        vt_sc[0:LANES, :] = v_ref[...]
        vt_sc[LANES:nd, :] = jnp.ones((L_ROWS, seq), BF16)

    qt = q_ref[...].T
    drow = lax.broadcasted_iota(I32, (LANES, tq), 0)
    head_rows = [drow < HEAD_DIM, drow >= HEAD_DIM]
    kmh = km_ref[...].astype(BF16)
    bidx = lax.broadcasted_iota(I32, (nb, tq), 0)
    key_i = lax.broadcasted_iota(I32, (blk, tq), 0)
    qry_i = lax.broadcasted_iota(I32, (blk, tq), 1)
    causal = key_i <= qry_i
    start = pl.multiple_of(qi * blk, blk)
    kd = kb_sc[pl.ds(start, blk), :]
    vd = vt_sc[:, pl.ds(start, blk)]
    for h in range(2):
        qh = jnp.where(head_rows[h], qt, 0.0)
        gate = _dot(kmh, qh.astype(BF16))
        gate = jnp.where(bidx < qi, gate, NEG)
        for r in range(MOBA_TOPK):
            mx = jnp.max(gate, axis=0, keepdims=True)
            ix = jnp.min(jnp.where(gate == mx, bidx, nb), axis=0, keepdims=True)
            gate = jnp.where(bidx == ix, -jnp.inf, gate)
            sel_sc[h * MOBA_TOPK + r] = jnp.where(r < qi, ix, -1)
        qb = (qh * (HEAD_DIM ** -0.5)).astype(BF16)
        qb_sc[h] = qb
        s = _dot(kd, qb)
        s = jnp.where(causal, s, NEG)
        m = jnp.max(s, axis=0, keepdims=True)
        p = jnp.exp(s - m)
        m_sc[h] = m
        acc_sc[h] = _dot(vd, p.astype(BF16))

    def body(g, carry):
        kjs, vjs, js = [], [], []
        for u in range(G):
            j = g * G + u
            st = pl.multiple_of(jnp.minimum(j, nb - 1) * blk, blk)
            kjs.append(kb_sc[pl.ds(st, blk), :])
            vjs.append(vt_sc[:, pl.ds(st, blk)])
            js.append(j)
        ss = [[_dot(kjs[u], qb_sc[h]) for u in range(G)] for h in range(2)]
        ps, alphas = [], []
        for h in range(2):
            b = h * MOBA_TOPK
            sm = []
            for u in range(G):
                rowsel = (sel_sc[b] == js[u]) | (sel_sc[b + 1] == js[u]) | (sel_sc[b + 2] == js[u])
                sm.append(jnp.where(rowsel, ss[h][u], NEG))
            m_old = m_sc[h]
            m_new = m_old
            for u in range(G):
                m_new = jnp.maximum(m_new, jnp.max(sm[u], axis=0, keepdims=True))
            alphas.append(jnp.exp(m_old - m_new))
            ps.append(jnp.concatenate([jnp.exp(sm[u] - m_new).astype(BF16) for u in range(G)], axis=0))
            m_sc[h] = m_new
        vcat = jnp.concatenate(vjs, axis=1)
        pv = [_dot(vcat, ps[h]) for h in range(2)]
        for h in range(2):
            acc_sc[h] = alphas[h] * acc_sc[h] + pv[h]
        return carry

    lax.fori_loop(0, (qi + (G - 1)) // G, body, 0)
    a0 = acc_sc[0]
    a1 = acc_sc[1]
    o0 = a0[0:LANES] / a0[LANES:LANES + 1]
    o1 = a1[0:LANES] / a1[LANES:LANES + 1]
    o_ref[...] = jnp.where(head_rows[0], o0, o1).T.astype(o_ref.dtype)


def moba_prompt(q, k, vt, kmean, *, batch, seq):
    tq = MOBA_BLOCK
    nq = seq // tq
    nb = seq // MOBA_BLOCK
    hp = D_GROUP // LANES
    q3 = q.reshape(batch, seq, D_GROUP)
    k3 = k.reshape(batch, seq, D_GROUP)
    km3 = kmean.reshape(batch, nb, D_GROUP)
    nd = LANES + L_ROWS
    out = pl.pallas_call(
        functools.partial(_moba_kernel, tq=tq, seq=seq),
        grid=(batch, hp, nq),
        in_specs=[pl.BlockSpec((None, tq, LANES), lambda b, p, i: (b, i, p)),
                  pl.BlockSpec((None, seq, LANES), lambda b, p, i: (b, 0, p)),
                  pl.BlockSpec((None, LANES, seq), lambda b, p, i: (b, p, 0)),
                  pl.BlockSpec((None, nb, LANES), lambda b, p, i: (b, 0, p))],
        out_specs=pl.BlockSpec((None, tq, LANES), lambda b, p, i: (b, i, p)),
        out_shape=jax.ShapeDtypeStruct((batch, seq, D_GROUP), BF16),
        scratch_shapes=[pltpu.VMEM((nd, seq), BF16),
                        pltpu.VMEM((2, LANES, tq), BF16), pltpu.VMEM((2, 1, tq), F32),
                        pltpu.VMEM((2, nd, tq), F32), pltpu.VMEM((2 * MOBA_TOPK, 1, tq), I32)],
        compiler_params=_cparams(("arbitrary", "arbitrary", "arbitrary"), VMEM_LIMIT),
        name="moba_prompt",
    )(q3, k3, vt, km3)
    return out.reshape(batch * seq, D_GROUP)


DEC_PG = 16
DEC_NEW = 16
DEC_COLS = 32


def _moba_decode_kernel(pt_ref, q2_ref, kn_ref, vn_ref, *rest, n_steps, page, n_q):
    k_refs = rest[:DEC_PG]
    v_refs = rest[DEC_PG:2 * DEC_PG]
    o_ref = rest[2 * DEC_PG]
    s_sc, gate_sc, acc_sc, l_sc, pown_sc = rest[2 * DEC_PG + 1:]
    s = pl.program_id(1)
    ppb = MOBA_BLOCK // page
    nblk = n_steps * DEC_PG // ppb
    nc = DEC_COLS
    scale = HEAD_DIM ** -0.5
    q2 = q2_ref[...]
    lane = lax.broadcasted_iota(I32, (nc, LANES), 1)

    @pl.when(s == 0)
    def _():
        gate_sc[...] = jnp.zeros(gate_sc.shape, F32)

    @pl.when(s < n_steps)
    def _():
        g = gate_sc[...]
        for i in range(DEC_PG):
            pidx = s * DEC_PG + i
            sf = _dot(q2, k_refs[i][...].astype(BF16))
            s_sc[:, pl.ds(pl.multiple_of(pidx * page, page), page)] = sf
            g = g + jnp.where(lane == pidx // ppb, jnp.sum(sf, axis=1, keepdims=True), 0.0)
        gate_sc[...] = g

    @pl.when(s == n_steps - 1)
    def _():
        gate = jnp.where(lane < nblk, gate_sc[...] * (1.0 / MOBA_BLOCK), -jnp.inf)
        sels = []
        for r in range(MOBA_TOPK):
            mx = jnp.max(gate, axis=1, keepdims=True)
            ix = jnp.min(jnp.where(gate == mx, lane, LANES), axis=1, keepdims=True)
            gate = jnp.where(lane == ix, -jnp.inf, gate)
            sels.append(ix)
        so = _dot_nt(q2, kn_ref[...].astype(BF16)) * scale
        key_c = lax.broadcasted_iota(I32, so.shape, 1)
        q_r = lax.broadcasted_iota(I32, so.shape, 0) // N_HEADS
        own_ok = key_c <= q_r
        so = jnp.where(own_ok, so, NEG)
        npos = s_sc.shape[1]
        blk_of = lax.broadcasted_iota(I32, (nc, npos), 1) // MOBA_BLOCK
        mask = (blk_of == sels[0]) | (blk_of == sels[1]) | (blk_of == sels[2])
        sc = jnp.where(mask, s_sc[...] * scale, NEG)
        m = jnp.maximum(jnp.max(sc, axis=1, keepdims=True), jnp.max(so, axis=1, keepdims=True))
        p = jnp.where(mask, jnp.exp(sc - m), 0.0)
        po = jnp.where(own_ok, jnp.exp(so - m), 0.0)
        s_sc[...] = p
        pown_sc[...] = po
        l_sc[...] = jnp.sum(p, axis=1, keepdims=True) + jnp.sum(po, axis=1, keepdims=True)
        acc_sc[...] = jnp.zeros(acc_sc.shape, F32)

    @pl.when(s >= n_steps)
    def _():
        acc = acc_sc[...]
        for i in range(DEC_PG):
            pidx = (s - n_steps) * DEC_PG + i
            pj = s_sc[:, pl.ds(pl.multiple_of(pidx * page, page), page)].astype(BF16)
            acc = acc + _dot_nt(pj, v_refs[i][...].astype(BF16))
        acc_sc[...] = acc

    @pl.when(s == 2 * n_steps - 1)
    def _():
        acc = acc_sc[...] + _dot(pown_sc[...].astype(BF16), vn_ref[...].astype(BF16))
        out = acc / l_sc[...]
        hrow = lax.broadcasted_iota(I32, (N_HEADS, D_GROUP), 0)
        hlane = lax.broadcasted_iota(I32, (N_HEADS, D_GROUP), 1) // HEAD_DIM
        rows = []
        for q in range(n_q):
            blk8 = out[q * N_HEADS:(q + 1) * N_HEADS, :]
            rows.append(jnp.sum(jnp.where(hrow == hlane, blk8, 0.0), axis=0, keepdims=True))
        rows.append(jnp.zeros((8 - n_q, D_GROUP), F32))
        o_ref[...] = jnp.concatenate(rows, axis=0)


def moba_decode(page_table, q2, k_new, v_new, cache_kt, cache_vt, *, layer, n_q):
    bsz, n_pages = page_table.shape
    page = cache_kt.shape[3]
    assert n_pages % DEC_PG == 0 and MOBA_BLOCK % page == 0 and page % LANES == 0
    assert n_q <= 8 and n_q * N_HEADS <= DEC_COLS and MOBA_TOPK <= n_pages * page // MOBA_BLOCK <= LANES
    n_steps = n_pages // DEC_PG
    kspec = lambda i: pl.BlockSpec(
        (None, None, D_GROUP, page),
        lambda b, s, pt, i=i: (layer, pt[b, jnp.minimum(s, n_steps - 1) * DEC_PG + i], 0, 0))
    vspec = lambda i: pl.BlockSpec(
        (None, None, D_GROUP, page),
        lambda b, s, pt, i=i: (layer, pt[b, jnp.maximum(s - n_steps, 0) * DEC_PG + i], 0, 0))
    perb = lambda r, w: pl.BlockSpec((None, r, w), lambda b, s, pt: (b, 0, 0))
    return pl.pallas_call(
        functools.partial(_moba_decode_kernel, n_steps=n_steps, page=page, n_q=n_q),
        grid_spec=pltpu.PrefetchScalarGridSpec(
            num_scalar_prefetch=1,
            grid=(bsz, 2 * n_steps),
            in_specs=[perb(DEC_COLS, D_GROUP), perb(DEC_NEW, D_GROUP), perb(DEC_NEW, D_GROUP)]
            + [kspec(i) for i in range(DEC_PG)] + [vspec(i) for i in range(DEC_PG)],
            out_specs=perb(8, D_GROUP),
            scratch_shapes=[pltpu.VMEM((DEC_COLS, n_pages * page), F32), pltpu.VMEM((DEC_COLS, LANES), F32),
                            pltpu.VMEM((DEC_COLS, D_GROUP), F32), pltpu.VMEM((DEC_COLS, 1), F32),
                            pltpu.VMEM((DEC_COLS, DEC_NEW), F32)]),
        out_shape=jax.ShapeDtypeStruct((bsz, 8, D_GROUP), F32),
        compiler_params=_cparams(("arbitrary", "arbitrary"), VMEM_LIMIT),
        name="moba_decode",
    )(page_table, q2, k_new, v_new, *([cache_kt] * DEC_PG), *([cache_vt] * DEC_PG))


def _log_sigmoid(x):
    return jnp.minimum(x, 0.0) - jnp.log1p(jnp.exp(-jnp.abs(x)))


def _mlstm_kernel(qk_ref, v_ref, om_ref, g_ref, cw_ref, cb_ref, bg_ref, gh_ref,
                  conv0_ref, c0_ref, n0_ref, m0_ref,
                  h_ref, cout_ref, nout_ref, mout_ref,
                  conv_sc, c_sc, n_sc, m_sc, *, L, t_valid):
    ci = pl.program_id(1)
    nh, dh, dg = N_HEADS, HEAD_DIM, D_GROUP

    @pl.when(ci == 0)
    def _():
        conv_sc[...] = conv0_ref[...]
        c_sc[...] = c0_ref[...]
        n_sc[...] = n0_ref[...]
        m_sc[...] = m0_ref[...]

    x = qk_ref[...]
    xc = jnp.concatenate([conv_sc[...], x], axis=0)
    y = cb_ref[...]
    for j in range(CONV_W):
        off = 8 - (CONV_W - 1) + j
        y = y + xc[off:off + L] * cw_ref[j:j + 1, :]
    conv_sc[...] = x[L - 8:L]
    qk = y * jax.nn.sigmoid(y)
    q_all = qk[:, :dg].astype(BF16)
    k_all = qk[:, dg:] * (dh ** -0.5)
    v_all = v_ref[...]
    om = om_ref[...]

    g = g_ref[...] + bg_ref[...]
    lf = _log_sigmoid(g)
    ri = lax.broadcasted_iota(I32, (L, L), 0)
    cj = lax.broadcasted_iota(I32, (L, L), 1)
    causal = cj <= ri
    tril = jnp.where(causal, 1.0, 0.0).astype(BF16)
    lf3 = _split3(lf)
    b_col = _dot(tril, lf3[0]) + _dot(tril, lf3[1]) + _dot(tril, lf3[2])
    sel_r = lax.broadcasted_iota(I32, (8, LANES), 0)
    sel_c = lax.broadcasted_iota(I32, (8, LANES), 1)
    sel_i = jnp.where(sel_c == sel_r, 1.0, 0.0).astype(BF16)
    sel_f = jnp.where(sel_c == sel_r + nh, 1.0, 0.0).astype(BF16)

    def rows_of(sel, a):
        a3 = _split3(a)
        return _dot_nt(sel, a3[0]) + _dot_nt(sel, a3[1]) + _dot_nt(sel, a3[2])

    li_row = rows_of(sel_i, g)
    b_row = rows_of(sel_f, b_col)
    rvalid = lax.broadcasted_iota(I32, (L, 1), 0) < t_valid
    heads = range(nh)
    hsl = [slice(h * dh, (h + 1) * dh) for h in heads]
    qs = [q_all[:, hsl[h]] for h in heads]
    ks = [k_all[:, hsl[h]] for h in heads]
    vs = [v_all[:, hsl[h]] for h in heads]
    cs = [c_sc[h] for h in heads]
    ns = [n_sc[h:h + 1, :] for h in heads]
    qk_raw = [_dot_nt(qs[h], ks[h].astype(BF16)) for h in heads]
    qc = [_dot(qs[h], cs[h].astype(BF16)) for h in heads]
    bcs, m_prevs, m_ts, w_inters, ss = [], [], [], [], []
    for h in heads:
        bc = b_col[:, nh + h:nh + h + 1]
        log_d = jnp.where(causal, bc - b_row[h:h + 1, :] + li_row[h:h + 1, :], -jnp.inf)
        m_prev = m_sc[h:h + 1, 0:1]
        log_inter = m_prev + bc
        m_t = jnp.maximum(log_inter, jnp.max(log_d, axis=-1, keepdims=True))
        dw = jnp.exp(log_d - m_t)
        bcs.append(bc)
        m_prevs.append(m_prev)
        m_ts.append(m_t)
        w_inters.append(jnp.exp(log_inter - m_t))
        ss.append(qk_raw[h] * dw)
    sv = [_dot(ss[h].astype(BF16), vs[h]) for h in heads]
    kws, decays, m_news = [], [], []
    for h in heads:
        m_new = m_ts[h][t_valid - 1:t_valid, :]
        b_last = bcs[h][t_valid - 1:t_valid, :]
        w_state = jnp.where(rvalid, jnp.exp(b_last - bcs[h] + g[:, h:h + 1] - m_new), 0.0)
        decays.append(jnp.exp(m_prevs[h] + b_last - m_new))
        kws.append(ks[h] * w_state)
        m_news.append(m_new)
    kv = [lax.dot_general(kws[h].astype(BF16), vs[h], _TN, preferred_element_type=F32) for h in heads]
    outs = []
    for h in heads:
        num = sv[h] + w_inters[h] * qc[h]
        qn = jnp.sum(qs[h].astype(F32) * ns[h], axis=-1, keepdims=True)
        den = jnp.sum(ss[h], axis=-1, keepdims=True) + w_inters[h] * qn
        hh = num / jnp.maximum(jnp.abs(den), jnp.exp(-m_ts[h]))
        c_sc[h] = decays[h] * cs[h] + kv[h]
        n_sc[h:h + 1, :] = decays[h] * ns[h] + jnp.sum(kws[h], axis=0, keepdims=True)
        m_sc[h:h + 1, :] = jnp.broadcast_to(m_news[h], (1, LANES))
        hn = hh * lax.rsqrt(jnp.mean(hh * hh, axis=-1, keepdims=True) + EPS) * gh_ref[:, hsl[h]]
        outs.append(hn * jax.nn.sigmoid(om[:, hsl[h]]))
    h_ref[...] = jnp.concatenate(outs, axis=-1).astype(h_ref.dtype)
    cout_ref[...] = c_sc[...]
    nout_ref[...] = n_sc[...]
    mout_ref[...] = m_sc[...]


def mlstm(qk, vm, om, gates, conv_w, conv_b, bg_pad, g_head, conv0, c0, n0, m0b, *, L, t_valid):
    b, t, _ = qk.shape
    nh, dh, dg = N_HEADS, HEAD_DIM, D_GROUP
    nc = t // L
    tok = lambda w: pl.BlockSpec((None, L, w), lambda i, c: (i, c, 0))
    full2 = lambda r, w: pl.BlockSpec((r, w), lambda i, c: (0, 0))
    return pl.pallas_call(
        functools.partial(_mlstm_kernel, L=L, t_valid=t_valid),
        grid=(b, nc),
        in_specs=[tok(2 * dg), tok(dg), tok(dg), tok(GATE_PAD),
                  full2(CONV_W, 2 * dg), full2(1, 2 * dg), full2(1, GATE_PAD), full2(1, dg),
                  pl.BlockSpec((None, 8, 2 * dg), lambda i, c: (i, 0, 0)),
                  pl.BlockSpec((None, nh, dh, dh), lambda i, c: (i, 0, 0, 0)),
                  pl.BlockSpec((None, nh, dh), lambda i, c: (i, 0, 0)),
                  pl.BlockSpec((None, nh, LANES), lambda i, c: (i, 0, 0))],
        out_specs=[tok(dg),
                   pl.BlockSpec((None, nh, dh, dh), lambda i, c: (i, 0, 0, 0)),
                   pl.BlockSpec((None, nh, dh), lambda i, c: (i, 0, 0)),
                   pl.BlockSpec((None, nh, LANES), lambda i, c: (i, 0, 0))],
        out_shape=[jax.ShapeDtypeStruct((b, t, dg), BF16),
                   jax.ShapeDtypeStruct((b, nh, dh, dh), F32),
                   jax.ShapeDtypeStruct((b, nh, dh), F32),
                   jax.ShapeDtypeStruct((b, nh, LANES), F32)],
        scratch_shapes=[pltpu.VMEM((8, 2 * dg), F32), pltpu.VMEM((nh, dh, dh), F32),
                        pltpu.VMEM((nh, dh), F32), pltpu.VMEM((nh, LANES), F32)],
        compiler_params=_cparams(("arbitrary", "arbitrary"), VMEM_LIMIT),
        name="mlstm",
    )(qk, vm, om, gates, conv_w, conv_b, bg_pad, g_head, conv0, c0, n0, m0b)


def _outproj_kernel(att_ref, hm_ref, x_ref, gt_ref, sc_ref, sh_ref, g_ref, wo_ref, wrh_ref, br_ref, *rest):
    xo_ref, h2_ref, ti_ref, tw_ref = rest[-4:]
    dg = D_GROUP
    mix = _dot(att_ref[...], wo_ref[0:dg, :]) + _dot(hm_ref[...], wo_ref[dg:2 * dg, :])
    xn = x_ref[...] + gt_ref[...] * mix
    xo_ref[...] = xn
    ms = jnp.mean(xn * xn, axis=-1, keepdims=True)
    y = xn * lax.rsqrt(ms + EPS) * g_ref[...]
    h2 = y * (1.0 + sc_ref[...]) + sh_ref[...]
    hh = h2.astype(BF16)
    h2_ref[...] = hh.astype(F32)
    logits = _dot(hh, wrh_ref[...]) + br_ref[...]
    col = lax.broadcasted_iota(I32, logits.shape, 1)
    lg = jnp.where(col < N_EXPERTS, logits, -jnp.inf)
    vals, idxs = [], []
    for _ in range(TOP_K):
        mx = jnp.max(lg, axis=-1, keepdims=True)
        ix = jnp.min(jnp.where(lg == mx, col, LANES), axis=-1, keepdims=True)
        lg = jnp.where(col == ix, -jnp.inf, lg)
        vals.append(mx)
        idxs.append(ix)
    es = [jnp.exp(v - vals[0]) for v in vals]
    den = es[0] + es[1] + es[2] + es[3]
    ti = jnp.zeros(logits.shape, I32)
    tw = jnp.zeros(logits.shape, F32)
    for r in range(TOP_K):
        ti = jnp.where(col == r, idxs[r], ti)
        tw = jnp.where(col == r, es[r] / den, tw)
    ti_ref[...] = ti
    tw_ref[...] = tw


def out_proj(att, hm, x, mod, g_ffn, w_out, wr_hi, br_pad, routed, *, n_routed, row0, tm, tiles_per_group):
    n, d = x.shape
    rm = mod.shape[1]
    dg = D_GROUP
    b0 = row0 // tm
    row = lambda i: (i, 0)
    rrow = lambda i: (b0 + i, 0)
    modspec = lambda c: pl.BlockSpec((None, rm, d), lambda i, c=c: (i // tiles_per_group, 0, c))
    const = lambda r, w: pl.BlockSpec((r, w), lambda i: (0, 0))
    in_specs = [pl.BlockSpec((tm, dg), row), pl.BlockSpec((tm, dg), row), pl.BlockSpec((tm, d), row),
                modspec(2), modspec(4), modspec(3),
                const(1, d), const(2 * dg, d), const(d, LANES), const(1, LANES)]
    args = [att, hm, x, mod, mod, mod, g_ffn, w_out, wr_hi, br_pad]
    aliases = {}
    if routed is not None:
        aliases = {len(args) + j: 1 + j for j in range(3)}
        in_specs += [pl.BlockSpec(memory_space=pl.ANY)] * 3
        args += list(routed)
    return pl.pallas_call(
        _outproj_kernel,
        grid=(n // tm,),
        in_specs=in_specs,
        out_specs=[pl.BlockSpec((tm, d), row), pl.BlockSpec((tm, d), rrow),
                   pl.BlockSpec((tm, LANES), rrow), pl.BlockSpec((tm, LANES), rrow)],
        out_shape=[jax.ShapeDtypeStruct((n, d), F32), jax.ShapeDtypeStruct((n_routed, d), F32),
                   jax.ShapeDtypeStruct((n_routed, LANES), I32), jax.ShapeDtypeStruct((n_routed, LANES), F32)],
        input_output_aliases=aliases,
        compiler_params=_cparams(("arbitrary",), VMEM_LIMIT),
        name="out_proj",
    )(*args)


def _rank_kernel(ti_ref, rank_ref, cnt_ref, base_sc):
    i = pl.program_id(0)

    @pl.when(i == 0)
    def _():
        base_sc[...] = jnp.zeros(base_sc.shape, F32)

    ti = ti_ref[...]
    tm = ti.shape[0]
    col = lax.broadcasted_iota(I32, ti.shape, 1)
    es = []
    oh = jnp.zeros(ti.shape, F32)
    for k in range(TOP_K):
        ek = jnp.sum(jnp.where(col == k, ti, 0), axis=-1, keepdims=True)
        es.append(ek)
        oh = oh + jnp.where(col == ek, 1.0, 0.0)
    ri = lax.broadcasted_iota(I32, (tm, tm), 0)
    ci = lax.broadcasted_iota(I32, (tm, tm), 1)
    tri = jnp.where(ci < ri, 1.0, 0.0).astype(BF16)
    rank = _dot(tri, oh.astype(BF16)) + base_sc[0:1, :]
    out = jnp.zeros(ti.shape, F32)
    for k in range(TOP_K):
        rk = jnp.sum(jnp.where(col == es[k], rank, 0.0), axis=-1, keepdims=True)
        out = jnp.where(col == k, rk, out)
    rank_ref[...] = out.astype(I32)
    tot = base_sc[0:1, :] + jnp.sum(oh, axis=0, keepdims=True)
    base_sc[...] = jnp.broadcast_to(tot, base_sc.shape)
    cnt_ref[...] = base_sc[...].astype(I32)


def expert_ranks(ti, *, tm):
    n = ti.shape[0]
    return pl.pallas_call(
        _rank_kernel,
        grid=(n // tm,),
        in_specs=[pl.BlockSpec((tm, LANES), lambda i: (i, 0))],
        out_specs=[pl.BlockSpec((tm, LANES), lambda i: (i, 0)), pl.BlockSpec((8, LANES), lambda i: (0, 0))],
        out_shape=[jax.ShapeDtypeStruct((n, LANES), I32), jax.ShapeDtypeStruct((8, LANES), I32)],
        scratch_shapes=[pltpu.VMEM((8, LANES), F32)],
        compiler_params=_cparams(("arbitrary",)),
        name="expert_ranks",
    )(ti)


def _moe_kernel(be_ref, first_ref, nused_ref, x_ref, wgu_ref, bgu_ref, wd_ref, bd_ref, y_ref, wgu_sc, wd_sc):
    i = pl.program_id(0)
    f = wd_ref.shape[0]

    @pl.when(first_ref[i] == 1)
    def _():
        wgu_sc[...] = wgu_ref[...].astype(BF16)
        wd_sc[...] = wd_ref[...].astype(BF16)

    @pl.when(i < nused_ref[0])
    def _():
        gu = _dot(x_ref[...].astype(BF16), wgu_sc[...]) + bgu_ref[...]
        gg = jnp.minimum(gu[:, :f], SWIGLU_LIMIT)
        uu = jnp.clip(gu[:, f:], -SWIGLU_LIMIT, SWIGLU_LIMIT)
        act = (uu + 1.0) * gg * jax.nn.sigmoid(gg * SWIGLU_ALPHA)
        y_ref[...] = _dot(act.astype(BF16), wd_sc[...]) + bd_ref[...]


def moe_mlp(blk_exp, blk_first, n_used, xg, w_gu, b_gu, w_down, b_down, *, layer, bm):
    m, d = xg.shape
    depth, e, _, f2 = w_gu.shape
    f = f2 // 2
    nblk = m // bm
    xmap = lambda i, be, fi, nu: (jnp.minimum(i, nu[0] - 1), 0)
    wmap = lambda i, be, fi, nu: (layer, be[i], 0, 0)
    return pl.pallas_call(
        _moe_kernel,
        grid_spec=pltpu.PrefetchScalarGridSpec(
            num_scalar_prefetch=3,
            grid=(nblk,),
            in_specs=[pl.BlockSpec((bm, d), xmap),
                      pl.BlockSpec((None, None, d, f2), wmap),
                      pl.BlockSpec((None, None, 1, f2), wmap),
                      pl.BlockSpec((None, None, f, d), wmap),
                      pl.BlockSpec((None, None, 1, d), wmap)],
            out_specs=pl.BlockSpec((bm, d), xmap),
            scratch_shapes=[pltpu.VMEM((d, f2), BF16), pltpu.VMEM((f, d), BF16)]),
        out_shape=jax.ShapeDtypeStruct((m, d), F32),
        compiler_params=_cparams(("arbitrary",), VMEM_LIMIT),
        name="moe_mlp",
    )(blk_exp, blk_first, n_used, xg, w_gu, b_gu.reshape(depth, e, 1, f2), w_down, b_down.reshape(depth, e, 1, d))


def _combine_kernel(y_ref, tw_ref, x_ref, gt_ref, o_ref):
    tw = tw_ref[...]
    ff = y_ref[0] * tw[:, 0:1]
    for k in range(1, TOP_K):
        ff = ff + y_ref[k] * tw[:, k:k + 1]
    o_ref[...] = x_ref[...] + gt_ref[...] * ff


def moe_combine(yg, tw, x, mod, *, row0, tm, tiles_per_group):
    n, d = x.shape
    rm = mod.shape[1]
    b0 = row0 // tm
    return pl.pallas_call(
        _combine_kernel,
        grid=(n // tm,),
        in_specs=[pl.BlockSpec((TOP_K, tm, d), lambda i: (0, b0 + i, 0)),
                  pl.BlockSpec((tm, LANES), lambda i: (b0 + i, 0)),
                  pl.BlockSpec((tm, d), lambda i: (i, 0)),
                  pl.BlockSpec((None, rm, d), lambda i: (i // tiles_per_group, 0, N_ADA - 1))],
        out_specs=pl.BlockSpec((tm, d), lambda i: (i, 0)),
        out_shape=jax.ShapeDtypeStruct((n, d), F32),
        compiler_params=_cparams(("arbitrary",), VMEM_LIMIT),
        name="moe_combine",
    )(yg, tw, x, mod)


def _rms_kernel(x_ref, g_ref, o_ref):
    x = x_ref[...]
    o_ref[...] = x * lax.rsqrt(jnp.mean(x * x, axis=-1, keepdims=True) + EPS) * g_ref[...]


def final_norm(x, g, *, tm):
    n, d = x.shape
    return pl.pallas_call(
        _rms_kernel,
        grid=(n // tm,),
        in_specs=[pl.BlockSpec((tm, d), lambda i: (i, 0)), pl.BlockSpec((1, d), lambda i: (0, 0))],
        out_specs=pl.BlockSpec((tm, d), lambda i: (i, 0)),
        out_shape=jax.ShapeDtypeStruct((n, d), F32),
        compiler_params=_cparams(("arbitrary",)),
        name="final_norm",
    )(x, g)


def _decode_q2(q):
    b, ts, _ = q.shape
    q4 = q.reshape(b, ts, N_HEADS, HEAD_DIM)
    eye = jnp.eye(N_HEADS, dtype=F32)
    qbd = q4[:, :, None, :, :] * eye[None, None, :, :, None]
    qbd = qbd.reshape(b, ts * N_HEADS, D_GROUP)
    qbd = jnp.pad(qbd, ((0, 0), (0, DEC_COLS - ts * N_HEADS), (0, 0)))
    return qbd.astype(BF16)


def _route(ti, bm):
    n = ti.shape[0]
    na = n * TOP_K
    tm = max(t for t in range(8, 513, 8) if n % t == 0)
    rank, cnt = expert_ranks(ti, tm=tm)
    counts = cnt[0, :N_EXPERTS]
    padded = ((counts + bm - 1) // bm) * bm
    pad_end = jnp.cumsum(padded)
    pad_start = pad_end - padded
    e4 = ti[:, :TOP_K]
    onehot = e4[:, :, None] == jnp.arange(N_EXPERTS, dtype=I32)[None, None, :]
    pos = jnp.sum(jnp.where(onehot, pad_start[None, None, :], 0), axis=-1) + rank[:, :TOP_K]
    nblk = -(-(na + N_EXPERTS * (bm - 1)) // bm)
    m = nblk * bm
    tok = jnp.broadcast_to(jnp.arange(n, dtype=I32)[:, None], (n, TOP_K))
    row_tok = jnp.zeros((m,), I32).at[pos.reshape(-1)].set(tok.reshape(-1), unique_indices=True)
    blk_start = jnp.arange(nblk, dtype=I32) * bm
    blk_exp = jnp.minimum(jnp.sum(blk_start[:, None] >= pad_end[None, :], axis=1), N_EXPERTS - 1).astype(I32)
    n_used = (pad_end[-1] // bm).astype(I32)
    blk_first = jnp.concatenate([jnp.ones((1,), I32), (blk_exp[1:] != blk_exp[:-1]).astype(I32)])
    blk_first = jnp.where(jnp.arange(nblk) < n_used, blk_first, 0).astype(I32)
    return row_tok, blk_exp, blk_first, n_used.reshape(1), pos


def _layer(l, depth, xp, xs, modp, mods, P, S, W, stacks):
    dg = D_GROUP
    Bp, T = P["batch"], P["seq"]
    Bs, Ts = S["batch"], S["seq"]
    d = xp.shape[1]
    tm = 512
    kst_p, vst_p = stacks
    qa, kst_p, vst_p, kb, vbt, qk, vm, om, gt, kmean = in_proj(
        xp, modp, W["g_mix"][l], W["w_in"][l], kst_p, vst_p, layer=l, depth=depth,
        tm=tm, tiles_per_group=T // tm, fresh=True)
    att_p = moba_prompt(qa, kb, vbt, kmean.reshape(-1, dg), batch=Bp, seq=T)
    Lp = 256
    hm_p, c_p, n_p, m_p = mlstm(
        qk.reshape(Bp, T, 2 * dg), vm.reshape(Bp, T, dg), om.reshape(Bp, T, dg), gt.reshape(Bp, T, GATE_PAD),
        W["conv_w"][l], W["conv_b"][l], W["bg_pad"][l], W["g_head"][l],
        jnp.zeros((Bp, 8, 2 * dg), F32), jnp.zeros((Bp, N_HEADS, HEAD_DIM, HEAD_DIM), F32),
        jnp.zeros((Bp, N_HEADS, HEAD_DIM), F32), jnp.zeros((Bp, N_HEADS, LANES), F32), L=Lp, t_valid=Lp)
    conv_p = qk.reshape(Bp, T, 2 * dg)[:, T - (CONV_W - 1):]
    np_ = Bp * T
    ntot = np_ + Bs * Ts
    xp1, *routed = out_proj(att_p, hm_p.reshape(np_, dg), xp, modp, W["g_ffn"][l], W["w_out"][l],
                            W["wr_hi"][l], W["br_pad"][l], None,
                            n_routed=ntot, row0=0, tm=tm, tiles_per_group=T // tm)
    ns = Bs * Ts
    qa_s, ka_s, va_s, _, _, qk_s, vm_s, om_s, gt_s = in_proj(
        xs, mods, W["g_mix"][l], W["w_in"][l], None, None, layer=l, depth=depth,
        tm=ns, tiles_per_group=1, fresh=False)
    newpad = lambda a: jnp.pad(a.reshape(Bs, Ts, dg), ((0, 0), (0, DEC_NEW - Ts), (0, 0)))
    att_s = moba_decode(S["page_table"], _decode_q2(qa_s.reshape(Bs, Ts, dg)), newpad(ka_s), newpad(va_s),
                        S["cache_kt"], S["cache_vt"], layer=l, n_q=Ts)
    att_s = att_s[:, :Ts].reshape(ns, dg).astype(BF16)
    Ls = 16
    padt = lambda a: jnp.pad(a.reshape(Bs, Ts, -1), ((0, 0), (0, Ls - Ts), (0, 0)))
    conv0 = jnp.pad(S["state_conv"][l], ((0, 0), (8 - (CONV_W - 1), 0), (0, 0)))
    m0b = jnp.broadcast_to(S["state_m"][l][:, :, None], (Bs, N_HEADS, LANES))
    hm_s, c_s, n_s, m_s = mlstm(padt(qk_s), padt(vm_s), padt(om_s), padt(gt_s),
                                W["conv_w"][l], W["conv_b"][l], W["bg_pad"][l], W["g_head"][l],
                                conv0, S["state_c"][l], S["state_n"][l], m0b, L=Ls, t_valid=Ts)
    hm_s = hm_s[:, :Ts].reshape(ns, dg)
    conv_s = qk_s.reshape(Bs, Ts, 2 * dg)[:, Ts - (CONV_W - 1):]
    xs1, h2, ti, tw = out_proj(att_s, hm_s, xs, mods, W["g_ffn"][l], W["w_out"][l],
                               W["wr_hi"][l], W["br_pad"][l], routed,
                               n_routed=ntot, row0=np_, tm=ns, tiles_per_group=1)
    row_tok, blk_exp, blk_first, n_used, pos = _route(ti, MOE_BM)
    xg = h2[row_tok]
    yb = moe_mlp(blk_exp, blk_first, n_used, xg, W["w_gu"], W["b_gu"], W["w_down"], W["b_down"], layer=l, bm=MOE_BM)
    yg = yb[pos.T.reshape(-1)].reshape(TOP_K, ntot, d)
    xp2 = moe_combine(yg, tw, xp1, modp, row0=0, tm=tm, tiles_per_group=T // tm)
    xs2 = moe_combine(yg, tw, xs1, mods, row0=np_, tm=ns, tiles_per_group=1)
    outs_p = (c_p, n_p, m_p[:, :, 0], conv_p)
    outs_s = (ka_s.reshape(Bs, Ts, N_HEADS, HEAD_DIM), va_s.reshape(Bs, Ts, N_HEADS, HEAD_DIM),
              c_s, n_s, m_s[:, :, 0], conv_s)
    return xp2, xs2, outs_p, outs_s, (kst_p, vst_p)


def kernel(x_prompt, x_sample, c_prompt, c_sample, cache_k, cache_v, page_table, state_c, state_n, state_m,
           state_conv, w_ada, b_ada, g_mix, w_in, b_gate, conv_w, conv_b, g_head, w_out, g_ffn, w_router,
           b_router, w_gu, b_gu, w_down, b_down, g_final):
    depth = w_ada.shape[0]
    Bp, T, d = x_prompt.shape
    Bs, Ts, _ = x_sample.shape
    dg = D_GROUP
    n_gate = 2 * N_HEADS
    w_in_pad = jnp.pad(w_in, ((0, 0), (0, 0), (0, GATE_PAD - n_gate))).astype(BF16)
    wr_pad = jnp.pad(w_router, ((0, 0), (0, 0), (0, LANES - N_EXPERTS)))
    wr_hi = wr_pad.astype(BF16)
    W = dict(
        g_mix=g_mix.reshape(depth, 1, d), w_in=w_in_pad,
        conv_w=conv_w, conv_b=conv_b.reshape(depth, 1, 2 * dg),
        bg_pad=jnp.pad(b_gate, ((0, 0), (0, GATE_PAD - n_gate))).reshape(depth, 1, GATE_PAD),
        g_head=g_head.reshape(depth, 1, dg), w_out=w_out.astype(BF16), g_ffn=g_ffn.reshape(depth, 1, d),
        wr_hi=wr_hi,
        br_pad=jnp.pad(b_router, ((0, 0), (0, LANES - N_EXPERTS))).reshape(depth, 1, LANES),
        w_gu=w_gu, b_gu=b_gu, w_down=w_down, b_down=b_down)
    n_pool, page = cache_k.shape[1], cache_k.shape[2]
    assert (page_table.shape[1] * page) % MOBA_BLOCK == 0
    pages_t = lambda c: c.transpose(0, 1, 3, 4, 2).reshape(depth, n_pool, dg, page)
    S = dict(batch=Bs, seq=Ts, page_table=page_table, cache_kt=pages_t(cache_k), cache_vt=pages_t(cache_v),
             state_c=state_c, state_n=state_n, state_m=state_m, state_conv=state_conv)
    P = dict(batch=Bp, seq=T)
    nc = Bp + Bs
    rpad = -(-nc // 8) * 8
    c_all = jnp.pad(jnp.concatenate([c_prompt, c_sample], axis=0), ((0, rpad - nc), (0, 0)))
    mod = ada_mod(c_all, w_ada, b_ada)
    xp = x_prompt.reshape(Bp * T, d)
    xs = x_sample.reshape(Bs * Ts, d)
    acc_p, acc_s = [], []
    stacks = (None, None)
    for l in range(depth):
        modp = mod[l, :Bp].reshape(Bp, 1, N_ADA * d)
        mods = jnp.repeat(mod[l, Bp:nc], Ts, axis=0).reshape(1, Bs * Ts, N_ADA * d)
        xp, xs, op, os_, stacks = _layer(l, depth, xp, xs, modp, mods, P, S, W, stacks)
        acc_p.append(op)
        acc_s.append(os_)
    y_p = final_norm(xp, g_final.reshape(1, d), tm=512).reshape(Bp, T, d)
    y_s = final_norm(xs, g_final.reshape(1, d), tm=Bs * Ts).reshape(Bs, Ts, d)
    stack = lambda acc, i: jnp.stack([a[i] for a in acc])
    rows = lambda a: a.reshape(depth, Bp, N_HEADS, HEAD_DIM, T).transpose(0, 1, 4, 2, 3)
    return ((y_p, y_s, rows(stacks[0]), rows(stacks[1])) + tuple(stack(acc_p, i) for i in range(4))
            + tuple(stack(acc_s, i) for i in range(6)))
```

```python
import functools

import jax
import jax.numpy as jnp
from jax import lax
from jax.experimental import pallas as pl
from jax.experimental.pallas import tpu as pltpu

F32 = jnp.float32
BF16 = jnp.bfloat16
I32 = jnp.int32
U32 = jnp.uint32

HEAD_DIM = 64
N_HEADS = 8
D_GROUP = N_HEADS * HEAD_DIM
MOBA_BLOCK = 256
MOBA_TOPK = 3
CONV_W = 4
N_EXPERTS = 32
TOP_K = 4
SWIGLU_ALPHA = 1.702
SWIGLU_LIMIT = 7.0
N_ADA = 6
EPS = 1e-6
NEG = -1e30
LANES = 128
GATE_PAD = LANES
MOE_BM = 512
VMEM_LIMIT = 56 * 1024 * 1024


def _cparams(sem, vmem=None):
    return pltpu.CompilerParams(dimension_semantics=sem, vmem_limit_bytes=vmem)


def _split2(x):
    hi = x.astype(BF16)
    lo = (x - hi.astype(F32)).astype(BF16)
    return hi, lo


def _split3(x):
    hi = x.astype(BF16)
    r = x - hi.astype(F32)
    mid = r.astype(BF16)
    lo = (r - mid.astype(F32)).astype(BF16)
    return hi, mid, lo


_NT = (((1,), (1,)), ((), ()))
_TN = (((0,), (0,)), ((), ()))


def _dot(a, b):
    return jnp.dot(a, b, preferred_element_type=F32)


def _dot_nt(a, b):
    return lax.dot_general(a, b, _NT, preferred_element_type=F32)


def _ada_kernel(c_ref, w_ref, b_ref, o_ref):
    c = c_ref[...]
    s = (c * jax.nn.sigmoid(c)).astype(BF16)
    o_ref[...] = _dot(s, w_ref[...].astype(BF16)) + b_ref[...]


def ada_mod(c_all, w_ada, b_ada):
    depth, d, nd = w_ada.shape
    r = c_all.shape[0]
    nj = nd // d
    return pl.pallas_call(
        _ada_kernel,
        grid=(depth, nj),
        in_specs=[pl.BlockSpec((r, d), lambda l, j: (0, 0)),
                  pl.BlockSpec((None, d, d), lambda l, j: (l, 0, j)),
                  pl.BlockSpec((None, 1, d), lambda l, j: (l, 0, j))],
        out_specs=pl.BlockSpec((None, r, d), lambda l, j: (l, 0, j)),
        out_shape=jax.ShapeDtypeStruct((depth, r, nd), F32),
        compiler_params=_cparams(("arbitrary", "arbitrary")),
        name="ada_mod",
    )(c_all, w_ada, b_ada.reshape(depth, 1, nd))


def _inproj_kernel(x_ref, g_ref, sc_ref, sh_ref, w_ref, *rest, fresh, aliased):
    if aliased:
        rest = rest[2:]
    qa_ref, ks_ref, vs_ref, kb_ref, vb_ref, qk_ref, vm_ref, om_ref, gt_ref = rest[:9]
    x = x_ref[...]
    ms = jnp.mean(x * x, axis=-1, keepdims=True)
    y = x * lax.rsqrt(ms + EPS) * g_ref[...]
    h = (y * (1.0 + sc_ref[...]) + sh_ref[...]).astype(BF16)
    dg = D_GROUP
    tm = x.shape[0]

    def seg(a, b):
        return _dot(h, w_ref[:, a:b])

    qa_ref[...] = seg(0, dg)
    ka = seg(dg, 2 * dg)
    va = seg(2 * dg, 3 * dg)
    kb_ref[...] = ka.astype(BF16)
    if fresh:
        ks_ref[...] = ka.T
        vt = va.T
        vs_ref[...] = vt
        vb_ref[...] = vt.astype(BF16)
        km_ref = rest[9]
        nblk = tm // MOBA_BLOCK
        rows = [jnp.sum(ka[i * MOBA_BLOCK:(i + 1) * MOBA_BLOCK], axis=0, keepdims=True) for i in range(nblk)]
        km_ref[...] = jnp.concatenate(rows, axis=0) * (1.0 / MOBA_BLOCK)
    else:
        ks_ref[...] = ka
        vs_ref[...] = va
        vb_ref[...] = va.astype(BF16)
    qk_ref[...] = seg(3 * dg, 5 * dg)
    vm_ref[...] = seg(5 * dg, 6 * dg).astype(BF16)
    om_ref[...] = seg(6 * dg, 7 * dg)
    gt_ref[...] = seg(7 * dg, 7 * dg + GATE_PAD)


def in_proj(x, mod, g, w_pad, kstack, vstack, *, layer, depth, tm, tiles_per_group, fresh):
    n, d = x.shape
    rm = mod.shape[1]
    nw = w_pad.shape[1]
    dg = D_GROUP
    grid = (n // tm,)
    row = lambda i: (i, 0)
    modspec = lambda c: pl.BlockSpec((None, rm, d), lambda i, c=c: (i // tiles_per_group, 0, c))
    if fresh:
        ngrp = n // (tm * tiles_per_group)
        seq = tm * tiles_per_group
        tmap = lambda i: (i // tiles_per_group, 0, i % tiles_per_group)
        kv_shape = jax.ShapeDtypeStruct((depth, ngrp, dg, seq), F32)
        kv_spec = pl.BlockSpec((None, None, dg, tm), lambda i: (layer,) + tmap(i))
        vb_shape = jax.ShapeDtypeStruct((ngrp, dg, seq), BF16)
        vb_spec = pl.BlockSpec((None, dg, tm), tmap)
    else:
        kv_shape = jax.ShapeDtypeStruct((n, dg), F32)
        kv_spec = pl.BlockSpec((tm, dg), row)
        vb_shape = jax.ShapeDtypeStruct((n, dg), BF16)
        vb_spec = pl.BlockSpec((tm, dg), row)
    out_shapes = [jax.ShapeDtypeStruct((n, dg), F32), kv_shape, kv_shape,
                  jax.ShapeDtypeStruct((n, dg), BF16), vb_shape,
                  jax.ShapeDtypeStruct((n, 2 * dg), F32), jax.ShapeDtypeStruct((n, dg), BF16),
                  jax.ShapeDtypeStruct((n, dg), F32), jax.ShapeDtypeStruct((n, GATE_PAD), F32)]
    out_specs = [pl.BlockSpec((tm, dg), row), kv_spec, kv_spec,
                 pl.BlockSpec((tm, dg), row), vb_spec,
                 pl.BlockSpec((tm, 2 * dg), row), pl.BlockSpec((tm, dg), row),
                 pl.BlockSpec((tm, dg), row), pl.BlockSpec((tm, GATE_PAD), row)]
    if fresh:
        nblk = tm // MOBA_BLOCK
        out_shapes.append(jax.ShapeDtypeStruct((n // tm, nblk, dg), F32))
        out_specs.append(pl.BlockSpec((None, nblk, dg), lambda i: (i, 0, 0)))
    in_specs = [pl.BlockSpec((tm, d), row),
                pl.BlockSpec((1, d), lambda i: (0, 0)),
                modspec(1), modspec(0),
                pl.BlockSpec((d, nw), lambda i: (0, 0))]
    args = [x, g, mod, mod, w_pad]
    aliased = kstack is not None
    aliases = {}
    if aliased:
        in_specs += [pl.BlockSpec(memory_space=pl.ANY)] * 2
        args += [kstack, vstack]
        aliases = {5: 1, 6: 2}
    return pl.pallas_call(
        functools.partial(_inproj_kernel, fresh=fresh, aliased=aliased),
        grid=grid,
        in_specs=in_specs,
        out_specs=out_specs,
        out_shape=out_shapes,
        input_output_aliases=aliases,
        compiler_params=_cparams(("arbitrary",), VMEM_LIMIT),
        name="in_proj",
    )(*args)


L_ROWS = 16
MOBA_G = 4


def _moba_kernel(q_ref, k_ref, v_ref, km_ref, o_ref, vt_sc, qb_sc, m_sc, acc_sc, sel_sc, *, tq, seq):
    qi = pl.program_id(2)
    nb = km_ref.shape[0]
    blk = MOBA_BLOCK
    nd = LANES + L_ROWS
    G = MOBA_G
    kb_sc = k_ref

    @pl.when(qi == 0)
    def _():
        vt_sc[0:LANES, :] = v_ref[...]
        vt_sc[LANES:nd, :] = jnp.ones((L_ROWS, seq), BF16)

    qt = q_ref[...].T
    drow = lax.broadcasted_iota(I32, (LANES, tq), 0)
    head_rows = [drow < HEAD_DIM, drow >= HEAD_DIM]
    kmh = km_ref[...].astype(BF16)
    bidx = lax.broadcasted_iota(I32, (nb, tq), 0)
    key_i = lax.broadcasted_iota(I32, (blk, tq), 0)
    qry_i = lax.broadcasted_iota(I32, (blk, tq), 1)
    causal = key_i <= qry_i
    start = pl.multiple_of(qi * blk, blk)
    kd = kb_sc[pl.ds(start, blk), :]
    vd = vt_sc[:, pl.ds(start, blk)]
    for h in range(2):
        qh = jnp.where(head_rows[h], qt, 0.0)
        gate = _dot(kmh, qh.astype(BF16))
        gate = jnp.where(bidx < qi, gate, NEG)
        for r in range(MOBA_TOPK):
            mx = jnp.max(gate, axis=0, keepdims=True)
            ix = jnp.min(jnp.where(gate == mx, bidx, nb), axis=0, keepdims=True)
            gate = jnp.where(bidx == ix, -jnp.inf, gate)
            sel_sc[h * MOBA_TOPK + r] = jnp.where(r < qi, ix, -1)
        qb = (qh * (HEAD_DIM ** -0.5)).astype(BF16)
        qb_sc[h] = qb
        s = _dot(kd, qb)
        s = jnp.where(causal, s, NEG)
        m = jnp.max(s, axis=0, keepdims=True)
        p = jnp.exp(s - m)
        m_sc[h] = m
        acc_sc[h] = _dot(vd, p.astype(BF16))

    def body(g, carry):
        kjs, vjs, js = [], [], []
        for u in range(G):
            j = g * G + u
            st = pl.multiple_of(jnp.minimum(j, nb - 1) * blk, blk)
            kjs.append(kb_sc[pl.ds(st, blk), :])
            vjs.append(vt_sc[:, pl.ds(st, blk)])
            js.append(j)
        ss = [[_dot(kjs[u], qb_sc[h]) for u in range(G)] for h in range(2)]
        ps, alphas = [], []
        for h in range(2):
            b = h * MOBA_TOPK
            sm = []
            for u in range(G):
                rowsel = (sel_sc[b] == js[u]) | (sel_sc[b + 1] == js[u]) | (sel_sc[b + 2] == js[u])
                sm.append(jnp.where(rowsel, ss[h][u], NEG))
            m_old = m_sc[h]
            m_new = m_old
            for u in range(G):
                m_new = jnp.maximum(m_new, jnp.max(sm[u], axis=0, keepdims=True))
            alphas.append(jnp.exp(m_old - m_new))
            ps.append(jnp.concatenate([jnp.exp(sm[u] - m_new).astype(BF16) for u in range(G)], axis=0))
            m_sc[h] = m_new
        vcat = jnp.concatenate(vjs, axis=1)
        pv = [_dot(vcat, ps[h]) for h in range(2)]
        for h in range(2):
            acc_sc[h] = alphas[h] * acc_sc[h] + pv[h]
        return carry

    lax.fori_loop(0, (qi + (G - 1)) // G, body, 0)
    a0 = acc_sc[0]
    a1 = acc_sc[1]
    o0 = a0[0:LANES] / a0[LANES:LANES + 1]
    o1 = a1[0:LANES] / a1[LANES:LANES + 1]
    o_ref[...] = jnp.where(head_rows[0], o0, o1).T.astype(o_ref.dtype)


def moba_prompt(q, k, vt, kmean, *, batch, seq):
    tq = MOBA_BLOCK
    nq = seq // tq
    nb = seq // MOBA_BLOCK
    hp = D_GROUP // LANES
    q3 = q.reshape(batch, seq, D_GROUP)
    k3 = k.reshape(batch, seq, D_GROUP)
    km3 = kmean.reshape(batch, nb, D_GROUP)
    nd = LANES + L_ROWS
    out = pl.pallas_call(
        functools.partial(_moba_kernel, tq=tq, seq=seq),
        grid=(batch, hp, nq),
        in_specs=[pl.BlockSpec((None, tq, LANES), lambda b, p, i: (b, i, p)),
                  pl.BlockSpec((None, seq, LANES), lambda b, p, i: (b, 0, p)),
                  pl.BlockSpec((None, LANES, seq), lambda b, p, i: (b, p, 0)),
                  pl.BlockSpec((None, nb, LANES), lambda b, p, i: (b, 0, p))],
        out_specs=pl.BlockSpec((None, tq, LANES), lambda b, p, i: (b, i, p)),
        out_shape=jax.ShapeDtypeStruct((batch, seq, D_GROUP), BF16),
        scratch_shapes=[pltpu.VMEM((nd, seq), BF16),
                        pltpu.VMEM((2, LANES, tq), BF16), pltpu.VMEM((2, 1, tq), F32),
                        pltpu.VMEM((2, nd, tq), F32), pltpu.VMEM((2 * MOBA_TOPK, 1, tq), I32)],
        compiler_params=_cparams(("arbitrary", "arbitrary", "arbitrary"), VMEM_LIMIT),
        name="moba_prompt",
    )(q3, k3, vt, km3)
    return out.reshape(batch * seq, D_GROUP)


DEC_PG = 16
DEC_NEW = 16
DEC_COLS = 32


def _moba_decode_kernel(pt_ref, q2_ref, kn_ref, vn_ref, *rest, n_steps, page, n_q):
    k_sets = (rest[:DEC_PG], rest[DEC_PG:2 * DEC_PG])
    v_sets = (rest[2 * DEC_PG:3 * DEC_PG], rest[3 * DEC_PG:4 * DEC_PG])
    o_ref = rest[4 * DEC_PG]
    s_sc, gate_sc, acc_sc, l_sc, pown_sc = rest[4 * DEC_PG + 1:]
    s = pl.program_id(1)
    ppb = MOBA_BLOCK // page
    nblk = n_steps * DEC_PG // ppb
    nc = DEC_COLS
    scale = HEAD_DIM ** -0.5
    q2 = q2_ref[...]
    lane = lax.broadcasted_iota(I32, (nc, LANES), 1)

    @pl.when(s == 0)
    def _():
        gate_sc[...] = jnp.zeros(gate_sc.shape, F32)

    def k_step(k_refs):
        g = gate_sc[...]
        for i in range(DEC_PG):
            pidx = s * DEC_PG + i
            sf = _dot(q2, k_refs[i][...].astype(BF16))
            s_sc[:, pl.ds(pl.multiple_of(pidx * page, page), page)] = sf
            g = g + jnp.where(lane == pidx // ppb, jnp.sum(sf, axis=1, keepdims=True), 0.0)
        gate_sc[...] = g

    for par in range(2):
        pl.when((s < n_steps) & (s % 2 == par))(functools.partial(k_step, k_sets[par]))

    @pl.when(s == n_steps - 1)
    def _():
        gate = jnp.where(lane < nblk, gate_sc[...] * (1.0 / MOBA_BLOCK), -jnp.inf)
        sels = []
        for r in range(MOBA_TOPK):
            mx = jnp.max(gate, axis=1, keepdims=True)
            ix = jnp.min(jnp.where(gate == mx, lane, LANES), axis=1, keepdims=True)
            gate = jnp.where(lane == ix, -jnp.inf, gate)
            sels.append(ix)
        so = _dot_nt(q2, kn_ref[...].astype(BF16)) * scale
        key_c = lax.broadcasted_iota(I32, so.shape, 1)
        q_r = lax.broadcasted_iota(I32, so.shape, 0) // N_HEADS
        own_ok = key_c <= q_r
        so = jnp.where(own_ok, so, NEG)
        npos = s_sc.shape[1]
        blk_of = lax.broadcasted_iota(I32, (nc, npos), 1) // MOBA_BLOCK
        mask = (blk_of == sels[0]) | (blk_of == sels[1]) | (blk_of == sels[2])
        sc = jnp.where(mask, s_sc[...] * scale, NEG)
        m = jnp.maximum(jnp.max(sc, axis=1, keepdims=True), jnp.max(so, axis=1, keepdims=True))
        p = jnp.where(mask, jnp.exp(sc - m), 0.0)
        po = jnp.where(own_ok, jnp.exp(so - m), 0.0)
        s_sc[...] = p
        pown_sc[...] = po
        l_sc[...] = jnp.sum(p, axis=1, keepdims=True) + jnp.sum(po, axis=1, keepdims=True)
        acc_sc[...] = jnp.zeros(acc_sc.shape, F32)

    def v_step(v_refs):
        acc = acc_sc[...]
        for i in range(DEC_PG):
            pidx = (s - n_steps) * DEC_PG + i
            pj = s_sc[:, pl.ds(pl.multiple_of(pidx * page, page), page)].astype(BF16)
            acc = acc + _dot_nt(pj, v_refs[i][...].astype(BF16))
        acc_sc[...] = acc

    for par in range(2):
        pl.when((s >= n_steps) & ((s - n_steps) % 2 == par))(functools.partial(v_step, v_sets[par]))

    @pl.when(s == 2 * n_steps - 1)
    def _():
        acc = acc_sc[...] + _dot(pown_sc[...].astype(BF16), vn_ref[...].astype(BF16))
        out = acc / l_sc[...]
        hrow = lax.broadcasted_iota(I32, (N_HEADS, D_GROUP), 0)
        hlane = lax.broadcasted_iota(I32, (N_HEADS, D_GROUP), 1) // HEAD_DIM
        rows = []
        for q in range(n_q):
            blk8 = out[q * N_HEADS:(q + 1) * N_HEADS, :]
            rows.append(jnp.sum(jnp.where(hrow == hlane, blk8, 0.0), axis=0, keepdims=True))
        rows.append(jnp.zeros((8 - n_q, D_GROUP), F32))
        o_ref[...] = jnp.concatenate(rows, axis=0)


def moba_decode(page_table, q2, k_new, v_new, cache_kt, cache_vt, *, layer, n_q):
    bsz, n_pages = page_table.shape
    page = cache_kt.shape[3]
    assert n_pages % DEC_PG == 0 and MOBA_BLOCK % page == 0 and page % LANES == 0
    assert n_q <= 8 and n_q * N_HEADS <= DEC_COLS and MOBA_TOPK <= n_pages * page // MOBA_BLOCK <= LANES
    n_steps = n_pages // DEC_PG
    assert n_steps % 2 == 0
    grp = (lambda u: jnp.clip(2 * ((u + 1) // 2), 0, n_steps - 2),
           lambda u: jnp.clip(2 * (u // 2) + 1, 1, n_steps - 1))
    kspec = lambda par, i: pl.BlockSpec(
        (None, None, D_GROUP, page),
        lambda b, s, pt: (layer, pt[b, grp[par](s) * DEC_PG + i], 0, 0))
    vspec = lambda par, i: pl.BlockSpec(
        (None, None, D_GROUP, page),
        lambda b, s, pt: (layer, pt[b, grp[par](s - n_steps) * DEC_PG + i], 0, 0))
    perb = lambda r, w: pl.BlockSpec((None, r, w), lambda b, s, pt: (b, 0, 0))
    return pl.pallas_call(
        functools.partial(_moba_decode_kernel, n_steps=n_steps, page=page, n_q=n_q),
        grid_spec=pltpu.PrefetchScalarGridSpec(
            num_scalar_prefetch=1,
            grid=(bsz, 2 * n_steps),
            in_specs=[perb(DEC_COLS, D_GROUP), perb(DEC_NEW, D_GROUP), perb(DEC_NEW, D_GROUP)]
            + [kspec(par, i) for par in range(2) for i in range(DEC_PG)]
            + [vspec(par, i) for par in range(2) for i in range(DEC_PG)],
            out_specs=perb(8, D_GROUP),
            scratch_shapes=[pltpu.VMEM((DEC_COLS, n_pages * page), F32), pltpu.VMEM((DEC_COLS, LANES), F32),
                            pltpu.VMEM((DEC_COLS, D_GROUP), F32), pltpu.VMEM((DEC_COLS, 1), F32),
                            pltpu.VMEM((DEC_COLS, DEC_NEW), F32)]),
        out_shape=jax.ShapeDtypeStruct((bsz, 8, D_GROUP), F32),
        compiler_params=_cparams(("arbitrary", "arbitrary"), VMEM_LIMIT),
        name="moba_decode",
    )(page_table, q2, k_new, v_new, *([cache_kt] * (2 * DEC_PG)), *([cache_vt] * (2 * DEC_PG)))


def _log_sigmoid(x):
    return jnp.minimum(x, 0.0) - jnp.log1p(jnp.exp(-jnp.abs(x)))


def _mlstm_kernel(qk_ref, v_ref, om_ref, g_ref, cw_ref, cb_ref, bg_ref, gh_ref,
                  conv0_ref, c0_ref, n0_ref, m0_ref,
                  h_ref, cout_ref, nout_ref, mout_ref,
                  conv_sc, c_sc, n_sc, m_sc, *, L, t_valid):
    ci = pl.program_id(1)
    nh, dh, dg = N_HEADS, HEAD_DIM, D_GROUP

    @pl.when(ci == 0)
    def _():
        conv_sc[...] = conv0_ref[...]
        c_sc[...] = c0_ref[...]
        n_sc[...] = n0_ref[...]
        m_sc[...] = m0_ref[...]

    x = qk_ref[...]
    xc = jnp.concatenate([conv_sc[...], x], axis=0)
    y = cb_ref[...]
    for j in range(CONV_W):
        off = 8 - (CONV_W - 1) + j
        y = y + xc[off:off + L] * cw_ref[j:j + 1, :]
    conv_sc[...] = x[L - 8:L]
    qk = y * jax.nn.sigmoid(y)
    q_all = qk[:, :dg].astype(BF16)
    k_all = qk[:, dg:] * (dh ** -0.5)
    v_all = v_ref[...]
    om = om_ref[...]

    g = g_ref[...] + bg_ref[...]
    lf = _log_sigmoid(g)
    ri = lax.broadcasted_iota(I32, (L, L), 0)
    cj = lax.broadcasted_iota(I32, (L, L), 1)
    causal = cj <= ri
    tril = jnp.where(causal, 1.0, 0.0).astype(BF16)
    lf3 = _split3(lf)
    b_col = _dot(tril, lf3[0]) + _dot(tril, lf3[1]) + _dot(tril, lf3[2])
    sel_r = lax.broadcasted_iota(I32, (8, LANES), 0)
    sel_c = lax.broadcasted_iota(I32, (8, LANES), 1)
    sel_i = jnp.where(sel_c == sel_r, 1.0, 0.0).astype(BF16)
    sel_f = jnp.where(sel_c == sel_r + nh, 1.0, 0.0).astype(BF16)

    def rows_of(sel, a):
        a3 = _split3(a)
        return _dot_nt(sel, a3[0]) + _dot_nt(sel, a3[1]) + _dot_nt(sel, a3[2])

    li_row = rows_of(sel_i, g)
    b_row = rows_of(sel_f, b_col)
    rvalid = lax.broadcasted_iota(I32, (L, 1), 0) < t_valid
    heads = range(nh)
    hsl = [slice(h * dh, (h + 1) * dh) for h in heads]
    qs = [q_all[:, hsl[h]] for h in heads]
    ks = [k_all[:, hsl[h]] for h in heads]
    vs = [v_all[:, hsl[h]] for h in heads]
    cs = [c_sc[h] for h in heads]
    ns = [n_sc[h:h + 1, :] for h in heads]
    qk_raw = [_dot_nt(qs[h], ks[h].astype(BF16)) for h in heads]
    qc = [_dot(qs[h], cs[h].astype(BF16)) for h in heads]
    bcs, m_prevs, m_ts, w_inters, ss = [], [], [], [], []
    for h in heads:
        bc = b_col[:, nh + h:nh + h + 1]
        log_d = jnp.where(causal, bc - b_row[h:h + 1, :] + li_row[h:h + 1, :], -jnp.inf)
        m_prev = m_sc[h:h + 1, 0:1]
        log_inter = m_prev + bc
        m_t = jnp.maximum(log_inter, jnp.max(log_d, axis=-1, keepdims=True))
        dw = jnp.exp(log_d - m_t)
        bcs.append(bc)
        m_prevs.append(m_prev)
        m_ts.append(m_t)
        w_inters.append(jnp.exp(log_inter - m_t))
        ss.append(qk_raw[h] * dw)
    sv = [_dot(ss[h].astype(BF16), vs[h]) for h in heads]
    kws, decays, m_news = [], [], []
    for h in heads:
        m_new = m_ts[h][t_valid - 1:t_valid, :]
        b_last = bcs[h][t_valid - 1:t_valid, :]
        w_state = jnp.where(rvalid, jnp.exp(b_last - bcs[h] + g[:, h:h + 1] - m_new), 0.0)
        decays.append(jnp.exp(m_prevs[h] + b_last - m_new))
        kws.append(ks[h] * w_state)
        m_news.append(m_new)
    kv = [lax.dot_general(kws[h].astype(BF16), vs[h], _TN, preferred_element_type=F32) for h in heads]
    outs = []
    for h in heads:
        num = sv[h] + w_inters[h] * qc[h]
        qn = jnp.sum(qs[h].astype(F32) * ns[h], axis=-1, keepdims=True)
        den = jnp.sum(ss[h], axis=-1, keepdims=True) + w_inters[h] * qn
        hh = num / jnp.maximum(jnp.abs(den), jnp.exp(-m_ts[h]))
        c_sc[h] = decays[h] * cs[h] + kv[h]
        n_sc[h:h + 1, :] = decays[h] * ns[h] + jnp.sum(kws[h], axis=0, keepdims=True)
        m_sc[h:h + 1, :] = jnp.broadcast_to(m_news[h], (1, LANES))
        hn = hh * lax.rsqrt(jnp.mean(hh * hh, axis=-1, keepdims=True) + EPS) * gh_ref[:, hsl[h]]
        outs.append(hn * jax.nn.sigmoid(om[:, hsl[h]]))
    h_ref[...] = jnp.concatenate(outs, axis=-1).astype(h_ref.dtype)
    cout_ref[...] = c_sc[...]
    nout_ref[...] = n_sc[...]
    mout_ref[...] = m_sc[...]


def mlstm(qk, vm, om, gates, conv_w, conv_b, bg_pad, g_head, conv0, c0, n0, m0b, *, L, t_valid):
    b, t, _ = qk.shape
    nh, dh, dg = N_HEADS, HEAD_DIM, D_GROUP
    nc = t // L
    tok = lambda w: pl.BlockSpec((None, L, w), lambda i, c: (i, c, 0))
    full2 = lambda r, w: pl.BlockSpec((r, w), lambda i, c: (0, 0))
    return pl.pallas_call(
        functools.partial(_mlstm_kernel, L=L, t_valid=t_valid),
        grid=(b, nc),
        in_specs=[tok(2 * dg), tok(dg), tok(dg), tok(GATE_PAD),
                  full2(CONV_W, 2 * dg), full2(1, 2 * dg), full2(1, GATE_PAD), full2(1, dg),
                  pl.BlockSpec((None, 8, 2 * dg), lambda i, c: (i, 0, 0)),
                  pl.BlockSpec((None, nh, dh, dh), lambda i, c: (i, 0, 0, 0)),
                  pl.BlockSpec((None, nh, dh), lambda i, c: (i, 0, 0)),
                  pl.BlockSpec((None, nh, LANES), lambda i, c: (i, 0, 0))],
        out_specs=[tok(dg),
                   pl.BlockSpec((None, nh, dh, dh), lambda i, c: (i, 0, 0, 0)),
                   pl.BlockSpec((None, nh, dh), lambda i, c: (i, 0, 0)),
                   pl.BlockSpec((None, nh, LANES), lambda i, c: (i, 0, 0))],
        out_shape=[jax.ShapeDtypeStruct((b, t, dg), BF16),
                   jax.ShapeDtypeStruct((b, nh, dh, dh), F32),
                   jax.ShapeDtypeStruct((b, nh, dh), F32),
                   jax.ShapeDtypeStruct((b, nh, LANES), F32)],
        scratch_shapes=[pltpu.VMEM((8, 2 * dg), F32), pltpu.VMEM((nh, dh, dh), F32),
                        pltpu.VMEM((nh, dh), F32), pltpu.VMEM((nh, LANES), F32)],
        compiler_params=_cparams(("arbitrary", "arbitrary"), VMEM_LIMIT),
        name="mlstm",
    )(qk, vm, om, gates, conv_w, conv_b, bg_pad, g_head, conv0, c0, n0, m0b)


def _outproj_kernel(att_ref, hm_ref, x_ref, gt_ref, sc_ref, sh_ref, g_ref, wo_ref, wrh_ref, br_ref, *rest):
    xo_ref, h2_ref, ti_ref, tw_ref = rest[-4:]
    dg = D_GROUP
    mix = _dot(att_ref[...], wo_ref[0:dg, :]) + _dot(hm_ref[...], wo_ref[dg:2 * dg, :])
    xn = x_ref[...] + gt_ref[...] * mix
    xo_ref[...] = xn
    ms = jnp.mean(xn * xn, axis=-1, keepdims=True)
    y = xn * lax.rsqrt(ms + EPS) * g_ref[...]
    h2 = y * (1.0 + sc_ref[...]) + sh_ref[...]
    hh = h2.astype(BF16)
    h2_ref[...] = hh.astype(F32)
    logits = _dot(hh, wrh_ref[...]) + br_ref[...]
    col = lax.broadcasted_iota(I32, logits.shape, 1)
    lg = jnp.where(col < N_EXPERTS, logits, -jnp.inf)
    vals, idxs = [], []
    for _ in range(TOP_K):
        mx = jnp.max(lg, axis=-1, keepdims=True)
        ix = jnp.min(jnp.where(lg == mx, col, LANES), axis=-1, keepdims=True)
        lg = jnp.where(col == ix, -jnp.inf, lg)
        vals.append(mx)
        idxs.append(ix)
    es = [jnp.exp(v - vals[0]) for v in vals]
    den = es[0] + es[1] + es[2] + es[3]
    ti = jnp.zeros(logits.shape, I32)
    tw = jnp.zeros(logits.shape, F32)
    for r in range(TOP_K):
        ti = jnp.where(col == r, idxs[r], ti)
        tw = jnp.where(col == r, es[r] / den, tw)
    ti_ref[...] = ti
    tw_ref[...] = tw


def out_proj(att, hm, x, mod, g_ffn, w_out, wr_hi, br_pad, routed, *, n_routed, row0, tm, tiles_per_group):
    n, d = x.shape
    rm = mod.shape[1]
    dg = D_GROUP
    b0 = row0 // tm
    row = lambda i: (i, 0)
    rrow = lambda i: (b0 + i, 0)
    modspec = lambda c: pl.BlockSpec((None, rm, d), lambda i, c=c: (i // tiles_per_group, 0, c))
    const = lambda r, w: pl.BlockSpec((r, w), lambda i: (0, 0))
    in_specs = [pl.BlockSpec((tm, dg), row), pl.BlockSpec((tm, dg), row), pl.BlockSpec((tm, d), row),
                modspec(2), modspec(4), modspec(3),
                const(1, d), const(2 * dg, d), const(d, LANES), const(1, LANES)]
    args = [att, hm, x, mod, mod, mod, g_ffn, w_out, wr_hi, br_pad]
    aliases = {}
    if routed is not None:
        aliases = {len(args) + j: 1 + j for j in range(3)}
        in_specs += [pl.BlockSpec(memory_space=pl.ANY)] * 3
        args += list(routed)
    return pl.pallas_call(
        _outproj_kernel,
        grid=(n // tm,),
        in_specs=in_specs,
        out_specs=[pl.BlockSpec((tm, d), row), pl.BlockSpec((tm, d), rrow),
                   pl.BlockSpec((tm, LANES), rrow), pl.BlockSpec((tm, LANES), rrow)],
        out_shape=[jax.ShapeDtypeStruct((n, d), F32), jax.ShapeDtypeStruct((n_routed, d), F32),
                   jax.ShapeDtypeStruct((n_routed, LANES), I32), jax.ShapeDtypeStruct((n_routed, LANES), F32)],
        input_output_aliases=aliases,
        compiler_params=_cparams(("arbitrary",), VMEM_LIMIT),
        name="out_proj",
    )(*args)


def _rank_kernel(ti_ref, rank_ref, cnt_ref, base_sc):
    i = pl.program_id(0)

    @pl.when(i == 0)
    def _():
        base_sc[...] = jnp.zeros(base_sc.shape, F32)

    ti = ti_ref[...]
    tm = ti.shape[0]
    col = lax.broadcasted_iota(I32, ti.shape, 1)
    es = []
    oh = jnp.zeros(ti.shape, F32)
    for k in range(TOP_K):
        ek = jnp.sum(jnp.where(col == k, ti, 0), axis=-1, keepdims=True)
        es.append(ek)
        oh = oh + jnp.where(col == ek, 1.0, 0.0)
    ri = lax.broadcasted_iota(I32, (tm, tm), 0)
    ci = lax.broadcasted_iota(I32, (tm, tm), 1)
    tri = jnp.where(ci < ri, 1.0, 0.0).astype(BF16)
    rank = _dot(tri, oh.astype(BF16)) + base_sc[0:1, :]
    out = jnp.zeros(ti.shape, F32)
    for k in range(TOP_K):
        rk = jnp.sum(jnp.where(col == es[k], rank, 0.0), axis=-1, keepdims=True)
        out = jnp.where(col == k, rk, out)
    rank_ref[...] = out.astype(I32)
    tot = base_sc[0:1, :] + jnp.sum(oh, axis=0, keepdims=True)
    base_sc[...] = jnp.broadcast_to(tot, base_sc.shape)
    cnt_ref[...] = base_sc[...].astype(I32)


def expert_ranks(ti, *, tm):
    n = ti.shape[0]
    return pl.pallas_call(
        _rank_kernel,
        grid=(n // tm,),
        in_specs=[pl.BlockSpec((tm, LANES), lambda i: (i, 0))],
        out_specs=[pl.BlockSpec((tm, LANES), lambda i: (i, 0)), pl.BlockSpec((8, LANES), lambda i: (0, 0))],
        out_shape=[jax.ShapeDtypeStruct((n, LANES), I32), jax.ShapeDtypeStruct((8, LANES), I32)],
        scratch_shapes=[pltpu.VMEM((8, LANES), F32)],
        compiler_params=_cparams(("arbitrary",)),
        name="expert_ranks",
    )(ti)


def _moe_kernel(be_ref, first_ref, nused_ref, x_ref, wgu_ref, bgu_ref, wd_ref, bd_ref, y_ref, wgu_sc, wd_sc):
    i = pl.program_id(0)
    f = wd_ref.shape[0]

    @pl.when(first_ref[i] == 1)
    def _():
        wgu_sc[...] = wgu_ref[...].astype(BF16)
        wd_sc[...] = wd_ref[...].astype(BF16)

    @pl.when(i < nused_ref[0])
    def _():
        gu = _dot(x_ref[...].astype(BF16), wgu_sc[...]) + bgu_ref[...]
        gg = jnp.minimum(gu[:, :f], SWIGLU_LIMIT)
        uu = jnp.clip(gu[:, f:], -SWIGLU_LIMIT, SWIGLU_LIMIT)
        act = (uu + 1.0) * gg * jax.nn.sigmoid(gg * SWIGLU_ALPHA)
        y_ref[...] = _dot(act.astype(BF16), wd_sc[...]) + bd_ref[...]


def moe_mlp(blk_exp, blk_first, n_used, xg, w_gu, b_gu, w_down, b_down, *, layer, bm):
    m, d = xg.shape
    depth, e, _, f2 = w_gu.shape
    f = f2 // 2
    nblk = m // bm
    xmap = lambda i, be, fi, nu: (jnp.minimum(i, nu[0] - 1), 0)
    wmap = lambda i, be, fi, nu: (layer, be[i], 0, 0)
    return pl.pallas_call(
        _moe_kernel,
        grid_spec=pltpu.PrefetchScalarGridSpec(
            num_scalar_prefetch=3,
            grid=(nblk,),
            in_specs=[pl.BlockSpec((bm, d), xmap),
                      pl.BlockSpec((None, None, d, f2), wmap),
                      pl.BlockSpec((None, None, 1, f2), wmap),
                      pl.BlockSpec((None, None, f, d), wmap),
                      pl.BlockSpec((None, None, 1, d), wmap)],
            out_specs=pl.BlockSpec((bm, d), xmap),
            scratch_shapes=[pltpu.VMEM((d, f2), BF16), pltpu.VMEM((f, d), BF16)]),
        out_shape=jax.ShapeDtypeStruct((m, d), F32),
        compiler_params=_cparams(("arbitrary",), VMEM_LIMIT),
        name="moe_mlp",
    )(blk_exp, blk_first, n_used, xg, w_gu, b_gu.reshape(depth, e, 1, f2), w_down, b_down.reshape(depth, e, 1, d))


def _combine_kernel(y_ref, tw_ref, x_ref, gt_ref, o_ref):
    tw = tw_ref[...]
    ff = y_ref[0] * tw[:, 0:1]
    for k in range(1, TOP_K):
        ff = ff + y_ref[k] * tw[:, k:k + 1]
    o_ref[...] = x_ref[...] + gt_ref[...] * ff


def moe_combine(yg, tw, x, mod, *, row0, tm, tiles_per_group):
    n, d = x.shape
    rm = mod.shape[1]
    b0 = row0 // tm
    return pl.pallas_call(
        _combine_kernel,
        grid=(n // tm,),
        in_specs=[pl.BlockSpec((TOP_K, tm, d), lambda i: (0, b0 + i, 0)),
                  pl.BlockSpec((tm, LANES), lambda i: (b0 + i, 0)),
                  pl.BlockSpec((tm, d), lambda i: (i, 0)),
                  pl.BlockSpec((None, rm, d), lambda i: (i // tiles_per_group, 0, N_ADA - 1))],
        out_specs=pl.BlockSpec((tm, d), lambda i: (i, 0)),
        out_shape=jax.ShapeDtypeStruct((n, d), F32),
        compiler_params=_cparams(("arbitrary",), VMEM_LIMIT),
        name="moe_combine",
    )(yg, tw, x, mod)


def _rms_kernel(x_ref, g_ref, o_ref):
    x = x_ref[...]
    o_ref[...] = x * lax.rsqrt(jnp.mean(x * x, axis=-1, keepdims=True) + EPS) * g_ref[...]


def final_norm(x, g, *, tm):
    n, d = x.shape
    return pl.pallas_call(
        _rms_kernel,
        grid=(n // tm,),
        in_specs=[pl.BlockSpec((tm, d), lambda i: (i, 0)), pl.BlockSpec((1, d), lambda i: (0, 0))],
        out_specs=pl.BlockSpec((tm, d), lambda i: (i, 0)),
        out_shape=jax.ShapeDtypeStruct((n, d), F32),
        compiler_params=_cparams(("arbitrary",)),
        name="final_norm",
    )(x, g)


def _decode_q2(q):
    b, ts, _ = q.shape
    q4 = q.reshape(b, ts, N_HEADS, HEAD_DIM)
    eye = jnp.eye(N_HEADS, dtype=F32)
    qbd = q4[:, :, None, :, :] * eye[None, None, :, :, None]
    qbd = qbd.reshape(b, ts * N_HEADS, D_GROUP)
    qbd = jnp.pad(qbd, ((0, 0), (0, DEC_COLS - ts * N_HEADS), (0, 0)))
    return qbd.astype(BF16)


def _route(ti, bm):
    n = ti.shape[0]
    na = n * TOP_K
    tm = max(t for t in range(8, 513, 8) if n % t == 0)
    rank, cnt = expert_ranks(ti, tm=tm)
    counts = cnt[0, :N_EXPERTS]
    padded = ((counts + bm - 1) // bm) * bm
    pad_end = jnp.cumsum(padded)
    pad_start = pad_end - padded
    e4 = ti[:, :TOP_K]
    onehot = e4[:, :, None] == jnp.arange(N_EXPERTS, dtype=I32)[None, None, :]
    pos = jnp.sum(jnp.where(onehot, pad_start[None, None, :], 0), axis=-1) + rank[:, :TOP_K]
    nblk = -(-(na + N_EXPERTS * (bm - 1)) // bm)
    m = nblk * bm
    tok = jnp.broadcast_to(jnp.arange(n, dtype=I32)[:, None], (n, TOP_K))
    row_tok = jnp.zeros((m,), I32).at[pos.reshape(-1)].set(tok.reshape(-1), unique_indices=True)
    blk_start = jnp.arange(nblk, dtype=I32) * bm
    blk_exp = jnp.minimum(jnp.sum(blk_start[:, None] >= pad_end[None, :], axis=1), N_EXPERTS - 1).astype(I32)
    n_used = (pad_end[-1] // bm).astype(I32)
    blk_first = jnp.concatenate([jnp.ones((1,), I32), (blk_exp[1:] != blk_exp[:-1]).astype(I32)])
    blk_first = jnp.where(jnp.arange(nblk) < n_used, blk_first, 0).astype(I32)
    return row_tok, blk_exp, blk_first, n_used.reshape(1), pos


def _layer(l, depth, xp, xs, modp, mods, P, S, W, stacks):
    dg = D_GROUP
    Bp, T = P["batch"], P["seq"]
    Bs, Ts = S["batch"], S["seq"]
    d = xp.shape[1]
    tm = 512
    kst_p, vst_p = stacks
    qa, kst_p, vst_p, kb, vbt, qk, vm, om, gt, kmean = in_proj(
        xp, modp, W["g_mix"][l], W["w_in"][l], kst_p, vst_p, layer=l, depth=depth,
        tm=tm, tiles_per_group=T // tm, fresh=True)
    att_p = moba_prompt(qa, kb, vbt, kmean.reshape(-1, dg), batch=Bp, seq=T)
    Lp = 256
    hm_p, c_p, n_p, m_p = mlstm(
        qk.reshape(Bp, T, 2 * dg), vm.reshape(Bp, T, dg), om.reshape(Bp, T, dg), gt.reshape(Bp, T, GATE_PAD),
        W["conv_w"][l], W["conv_b"][l], W["bg_pad"][l], W["g_head"][l],
        jnp.zeros((Bp, 8, 2 * dg), F32), jnp.zeros((Bp, N_HEADS, HEAD_DIM, HEAD_DIM), F32),
        jnp.zeros((Bp, N_HEADS, HEAD_DIM), F32), jnp.zeros((Bp, N_HEADS, LANES), F32), L=Lp, t_valid=Lp)
    conv_p = qk.reshape(Bp, T, 2 * dg)[:, T - (CONV_W - 1):]
    np_ = Bp * T
    ntot = np_ + Bs * Ts
    xp1, *routed = out_proj(att_p, hm_p.reshape(np_, dg), xp, modp, W["g_ffn"][l], W["w_out"][l],
                            W["wr_hi"][l], W["br_pad"][l], None,
                            n_routed=ntot, row0=0, tm=tm, tiles_per_group=T // tm)
    ns = Bs * Ts
    qa_s, ka_s, va_s, _, _, qk_s, vm_s, om_s, gt_s = in_proj(
        xs, mods, W["g_mix"][l], W["w_in"][l], None, None, layer=l, depth=depth,
        tm=ns, tiles_per_group=1, fresh=False)
    newpad = lambda a: jnp.pad(a.reshape(Bs, Ts, dg), ((0, 0), (0, DEC_NEW - Ts), (0, 0)))
    att_s = moba_decode(S["page_table"], _decode_q2(qa_s.reshape(Bs, Ts, dg)), newpad(ka_s), newpad(va_s),
                        S["cache_kt"], S["cache_vt"], layer=l, n_q=Ts)
    att_s = att_s[:, :Ts].reshape(ns, dg).astype(BF16)
    Ls = 16
    padt = lambda a: jnp.pad(a.reshape(Bs, Ts, -1), ((0, 0), (0, Ls - Ts), (0, 0)))
    conv0 = jnp.pad(S["state_conv"][l], ((0, 0), (8 - (CONV_W - 1), 0), (0, 0)))
    m0b = jnp.broadcast_to(S["state_m"][l][:, :, None], (Bs, N_HEADS, LANES))
    hm_s, c_s, n_s, m_s = mlstm(padt(qk_s), padt(vm_s), padt(om_s), padt(gt_s),
                                W["conv_w"][l], W["conv_b"][l], W["bg_pad"][l], W["g_head"][l],
                                conv0, S["state_c"][l], S["state_n"][l], m0b, L=Ls, t_valid=Ts)
    hm_s = hm_s[:, :Ts].reshape(ns, dg)
    conv_s = qk_s.reshape(Bs, Ts, 2 * dg)[:, Ts - (CONV_W - 1):]
    xs1, h2, ti, tw = out_proj(att_s, hm_s, xs, mods, W["g_ffn"][l], W["w_out"][l],
                               W["wr_hi"][l], W["br_pad"][l], routed,
                               n_routed=ntot, row0=np_, tm=ns, tiles_per_group=1)
    row_tok, blk_exp, blk_first, n_used, pos = _route(ti, MOE_BM)
    xg = h2[row_tok]
    yb = moe_mlp(blk_exp, blk_first, n_used, xg, W["w_gu"], W["b_gu"], W["w_down"], W["b_down"], layer=l, bm=MOE_BM)
    yg = yb[pos.T.reshape(-1)].reshape(TOP_K, ntot, d)
    xp2 = moe_combine(yg, tw, xp1, modp, row0=0, tm=tm, tiles_per_group=T // tm)
    xs2 = moe_combine(yg, tw, xs1, mods, row0=np_, tm=ns, tiles_per_group=1)
    outs_p = (c_p, n_p, m_p[:, :, 0], conv_p)
    outs_s = (ka_s.reshape(Bs, Ts, N_HEADS, HEAD_DIM), va_s.reshape(Bs, Ts, N_HEADS, HEAD_DIM),
              c_s, n_s, m_s[:, :, 0], conv_s)
    return xp2, xs2, outs_p, outs_s, (kst_p, vst_p)


def kernel(x_prompt, x_sample, c_prompt, c_sample, cache_k, cache_v, page_table, state_c, state_n, state_m,
           state_conv, w_ada, b_ada, g_mix, w_in, b_gate, conv_w, conv_b, g_head, w_out, g_ffn, w_router,
           b_router, w_gu, b_gu, w_down, b_down, g_final):
    depth = w_ada.shape[0]
    Bp, T, d = x_prompt.shape
    Bs, Ts, _ = x_sample.shape
    dg = D_GROUP
    n_gate = 2 * N_HEADS
    w_in_pad = jnp.pad(w_in, ((0, 0), (0, 0), (0, GATE_PAD - n_gate))).astype(BF16)
    wr_pad = jnp.pad(w_router, ((0, 0), (0, 0), (0, LANES - N_EXPERTS)))
    wr_hi = wr_pad.astype(BF16)
    W = dict(
        g_mix=g_mix.reshape(depth, 1, d), w_in=w_in_pad,
        conv_w=conv_w, conv_b=conv_b.reshape(depth, 1, 2 * dg),
        bg_pad=jnp.pad(b_gate, ((0, 0), (0, GATE_PAD - n_gate))).reshape(depth, 1, GATE_PAD),
        g_head=g_head.reshape(depth, 1, dg), w_out=w_out.astype(BF16), g_ffn=g_ffn.reshape(depth, 1, d),
        wr_hi=wr_hi,
        br_pad=jnp.pad(b_router, ((0, 0), (0, LANES - N_EXPERTS))).reshape(depth, 1, LANES),
        w_gu=w_gu, b_gu=b_gu, w_down=w_down, b_down=b_down)
    n_pool, page = cache_k.shape[1], cache_k.shape[2]
    assert (page_table.shape[1] * page) % MOBA_BLOCK == 0
    pages_t = lambda c: c.transpose(0, 1, 3, 4, 2).reshape(depth, n_pool, dg, page)
    S = dict(batch=Bs, seq=Ts, page_table=page_table, cache_kt=pages_t(cache_k), cache_vt=pages_t(cache_v),
             state_c=state_c, state_n=state_n, state_m=state_m, state_conv=state_conv)
    P = dict(batch=Bp, seq=T)
    nc = Bp + Bs
    rpad = -(-nc // 8) * 8
    c_all = jnp.pad(jnp.concatenate([c_prompt, c_sample], axis=0), ((0, rpad - nc), (0, 0)))
    mod = ada_mod(c_all, w_ada, b_ada)
    xp = x_prompt.reshape(Bp * T, d)
    xs = x_sample.reshape(Bs * Ts, d)
    acc_p, acc_s = [], []
    stacks = (None, None)
    for l in range(depth):
        modp = mod[l, :Bp].reshape(Bp, 1, N_ADA * d)
        mods = jnp.repeat(mod[l, Bp:nc], Ts, axis=0).reshape(1, Bs * Ts, N_ADA * d)
        xp, xs, op, os_, stacks = _layer(l, depth, xp, xs, modp, mods, P, S, W, stacks)
        acc_p.append(op)
        acc_s.append(os_)
    y_p = final_norm(xp, g_final.reshape(1, d), tm=512).reshape(Bp, T, d)
    y_s = final_norm(xs, g_final.reshape(1, d), tm=Bs * Ts).reshape(Bs, Ts, d)
    stack = lambda acc, i: jnp.stack([a[i] for a in acc])
    rows = lambda a: a.reshape(depth, Bp, N_HEADS, HEAD_DIM, T).transpose(0, 1, 4, 2, 3)
    return ((y_p, y_s, rows(stacks[0]), rows(stacks[1])) + tuple(stack(acc_p, i) for i in range(4))
            + tuple(stack(acc_s, i) for i in range(6)))
```

```python
import functools

import jax
import jax.numpy as jnp
from jax import lax
from jax.experimental import pallas as pl
from jax.experimental.pallas import tpu as pltpu

F32 = jnp.float32
BF16 = jnp.bfloat16
I32 = jnp.int32
U32 = jnp.uint32

HEAD_DIM = 64
N_HEADS = 8
D_GROUP = N_HEADS * HEAD_DIM
MOBA_BLOCK = 256
MOBA_TOPK = 3
CONV_W = 4
N_EXPERTS = 32
TOP_K = 4
SWIGLU_ALPHA = 1.702
SWIGLU_LIMIT = 7.0
N_ADA = 6
EPS = 1e-6
NEG = -1e30
LANES = 128
GATE_PAD = LANES
MOE_BM = 512
VMEM_LIMIT = 56 * 1024 * 1024


def _cparams(sem, vmem=None):
    return pltpu.CompilerParams(dimension_semantics=sem, vmem_limit_bytes=vmem)


def _split2(x):
    hi = x.astype(BF16)
    lo = (x - hi.astype(F32)).astype(BF16)
    return hi, lo


def _split3(x):
    hi = x.astype(BF16)
    r = x - hi.astype(F32)
    mid = r.astype(BF16)
    lo = (r - mid.astype(F32)).astype(BF16)
    return hi, mid, lo


_NT = (((1,), (1,)), ((), ()))
_TN = (((0,), (0,)), ((), ()))


def _dot(a, b):
    return jnp.dot(a, b, preferred_element_type=F32)


def _dot_nt(a, b):
    return lax.dot_general(a, b, _NT, preferred_element_type=F32)


def _ada_kernel(c_ref, w_ref, b_ref, o_ref):
    c = c_ref[...]
    s = (c * jax.nn.sigmoid(c)).astype(BF16)
    o_ref[...] = _dot(s, w_ref[...].astype(BF16)) + b_ref[...]


def ada_mod(c_all, w_ada, b_ada):
    depth, d, nd = w_ada.shape
    r = c_all.shape[0]
    nj = nd // d
    return pl.pallas_call(
        _ada_kernel,
        grid=(depth, nj),
        in_specs=[pl.BlockSpec((r, d), lambda l, j: (0, 0)),
                  pl.BlockSpec((None, d, d), lambda l, j: (l, 0, j)),
                  pl.BlockSpec((None, 1, d), lambda l, j: (l, 0, j))],
        out_specs=pl.BlockSpec((None, r, d), lambda l, j: (l, 0, j)),
        out_shape=jax.ShapeDtypeStruct((depth, r, nd), F32),
        compiler_params=_cparams(("arbitrary", "arbitrary")),
        name="ada_mod",
    )(c_all, w_ada, b_ada.reshape(depth, 1, nd))


def _inproj_kernel(x_ref, g_ref, sc_ref, sh_ref, w_ref, *rest, fresh, aliased):
    if aliased:
        rest = rest[2:]
    qa_ref, ks_ref, vs_ref, kb_ref, vb_ref, qk_ref, vm_ref, om_ref, gt_ref = rest[:9]
    x = x_ref[...]
    ms = jnp.mean(x * x, axis=-1, keepdims=True)
    y = x * lax.rsqrt(ms + EPS) * g_ref[...]
    h = (y * (1.0 + sc_ref[...]) + sh_ref[...]).astype(BF16)
    dg = D_GROUP
    tm = x.shape[0]

    def seg(a, b):
        return _dot(h, w_ref[:, a:b])

    qa_ref[...] = seg(0, dg)
    ka = seg(dg, 2 * dg)
    va = seg(2 * dg, 3 * dg)
    kb_ref[...] = ka.astype(BF16)
    if fresh:
        ks_ref[...] = ka.T
        vt = va.T
        vs_ref[...] = vt
        vb_ref[...] = vt.astype(BF16)
        km_ref = rest[9]
        nblk = tm // MOBA_BLOCK
        rows = [jnp.sum(ka[i * MOBA_BLOCK:(i + 1) * MOBA_BLOCK], axis=0, keepdims=True) for i in range(nblk)]
        km_ref[...] = jnp.concatenate(rows, axis=0) * (1.0 / MOBA_BLOCK)
    else:
        ks_ref[...] = ka
        vs_ref[...] = va
        vb_ref[...] = va.astype(BF16)
    qk_ref[...] = seg(3 * dg, 5 * dg)
    vm_ref[...] = seg(5 * dg, 6 * dg).astype(BF16)
    om_ref[...] = seg(6 * dg, 7 * dg)
    gt_ref[...] = seg(7 * dg, 7 * dg + GATE_PAD)


def in_proj(x, mod, g, w_pad, kstack, vstack, *, layer, depth, tm, tiles_per_group, fresh):
    n, d = x.shape
    rm = mod.shape[1]
    nw = w_pad.shape[1]
    dg = D_GROUP
    grid = (n // tm,)
    row = lambda i: (i, 0)
    modspec = lambda c: pl.BlockSpec((None, rm, d), lambda i, c=c: (i // tiles_per_group, 0, c))
    if fresh:
        ngrp = n // (tm * tiles_per_group)
        seq = tm * tiles_per_group
        tmap = lambda i: (i // tiles_per_group, 0, i % tiles_per_group)
        kv_shape = jax.ShapeDtypeStruct((depth, ngrp, dg, seq), F32)
        kv_spec = pl.BlockSpec((None, None, dg, tm), lambda i: (layer,) + tmap(i))
        vb_shape = jax.ShapeDtypeStruct((ngrp, dg, seq), BF16)
        vb_spec = pl.BlockSpec((None, dg, tm), tmap)
    else:
        kv_shape = jax.ShapeDtypeStruct((n, dg), F32)
        kv_spec = pl.BlockSpec((tm, dg), row)
        vb_shape = jax.ShapeDtypeStruct((n, dg), BF16)
        vb_spec = pl.BlockSpec((tm, dg), row)
    out_shapes = [jax.ShapeDtypeStruct((n, dg), F32), kv_shape, kv_shape,
                  jax.ShapeDtypeStruct((n, dg), BF16), vb_shape,
                  jax.ShapeDtypeStruct((n, 2 * dg), F32), jax.ShapeDtypeStruct((n, dg), BF16),
                  jax.ShapeDtypeStruct((n, dg), F32), jax.ShapeDtypeStruct((n, GATE_PAD), F32)]
    out_specs = [pl.BlockSpec((tm, dg), row), kv_spec, kv_spec,
                 pl.BlockSpec((tm, dg), row), vb_spec,
                 pl.BlockSpec((tm, 2 * dg), row), pl.BlockSpec((tm, dg), row),
                 pl.BlockSpec((tm, dg), row), pl.BlockSpec((tm, GATE_PAD), row)]
    if fresh:
        nblk = tm // MOBA_BLOCK
        out_shapes.append(jax.ShapeDtypeStruct((n // tm, nblk, dg), F32))
        out_specs.append(pl.BlockSpec((None, nblk, dg), lambda i: (i, 0, 0)))
    in_specs = [pl.BlockSpec((tm, d), row),
                pl.BlockSpec((1, d), lambda i: (0, 0)),
                modspec(1), modspec(0),
                pl.BlockSpec((d, nw), lambda i: (0, 0))]
    args = [x, g, mod, mod, w_pad]
    aliased = kstack is not None
    aliases = {}
    if aliased:
        in_specs += [pl.BlockSpec(memory_space=pl.ANY)] * 2
        args += [kstack, vstack]
        aliases = {5: 1, 6: 2}
    return pl.pallas_call(
        functools.partial(_inproj_kernel, fresh=fresh, aliased=aliased),
        grid=grid,
        in_specs=in_specs,
        out_specs=out_specs,
        out_shape=out_shapes,
        input_output_aliases=aliases,
        compiler_params=_cparams(("arbitrary",), VMEM_LIMIT),
        name="in_proj",
    )(*args)


L_ROWS = 16
MOBA_G = 4


def _moba_kernel(q_ref, k_ref, v_ref, km_ref, o_ref, vt_sc, qb_sc, m_sc, acc_sc, sel_sc, *, tq, seq):
    qi = pl.program_id(2)
    nb = km_ref.shape[0]
    blk = MOBA_BLOCK
    nd = LANES + L_ROWS
    G = MOBA_G
    kb_sc = k_ref

    @pl.when(qi == 0)
    def _():
        vt_sc[0:LANES, :] = v_ref[...]
        vt_sc[LANES:nd, :] = jnp.ones((L_ROWS, seq), BF16)

    qt = q_ref[...].T
    drow = lax.broadcasted_iota(I32, (LANES, tq), 0)
    head_rows = [drow < HEAD_DIM, drow >= HEAD_DIM]
    kmh = km_ref[...].astype(BF16)
    bidx = lax.broadcasted_iota(I32, (nb, tq), 0)
    key_i = lax.broadcasted_iota(I32, (blk, tq), 0)
    qry_i = lax.broadcasted_iota(I32, (blk, tq), 1)
    causal = key_i <= qry_i
    start = pl.multiple_of(qi * blk, blk)
    kd = kb_sc[pl.ds(start, blk), :]
    vd = vt_sc[:, pl.ds(start, blk)]
    for h in range(2):
        qh = jnp.where(head_rows[h], qt, 0.0)
        gate = _dot(kmh, qh.astype(BF16))
        gate = jnp.where(bidx < qi, gate, NEG)
        for r in range(MOBA_TOPK):
            mx = jnp.max(gate, axis=0, keepdims=True)
            ix = jnp.min(jnp.where(gate == mx, bidx, nb), axis=0, keepdims=True)
            gate = jnp.where(bidx == ix, -jnp.inf, gate)
            sel_sc[h * MOBA_TOPK + r] = jnp.where(r < qi, ix, -1)
        qb = (qh * (HEAD_DIM ** -0.5)).astype(BF16)
        qb_sc[h] = qb
        s = _dot(kd, qb)
        s = jnp.where(causal, s, NEG)
        m = jnp.max(s, axis=0, keepdims=True)
        p = jnp.exp(s - m)
        m_sc[h] = m
        acc_sc[h] = _dot(jnp.concatenate([vd[h * HEAD_DIM:(h + 1) * HEAD_DIM], vd[LANES:]], axis=0), p.astype(BF16))

    def body(g, carry):
        kjs, vjs, js = [], [], []
        for u in range(G):
            j = g * G + u
            st = pl.multiple_of(jnp.minimum(j, nb - 1) * blk, blk)
            kjs.append(kb_sc[pl.ds(st, blk), :])
            vjs.append(vt_sc[:, pl.ds(st, blk)])
            js.append(j)
        ss = [[_dot(kjs[u], qb_sc[h]) for u in range(G)] for h in range(2)]
        ps, alphas = [], []
        for h in range(2):
            b = h * MOBA_TOPK
            sm = []
            for u in range(G):
                rowsel = (sel_sc[b] == js[u]) | (sel_sc[b + 1] == js[u]) | (sel_sc[b + 2] == js[u])
                sm.append(jnp.where(rowsel, ss[h][u], NEG))
            m_old = m_sc[h]
            m_new = m_old
            for u in range(G):
                m_new = jnp.maximum(m_new, jnp.max(sm[u], axis=0, keepdims=True))
            alphas.append(jnp.exp(m_old - m_new))
            ps.append(jnp.concatenate([jnp.exp(sm[u] - m_new).astype(BF16) for u in range(G)], axis=0))
            m_sc[h] = m_new
        vcat = jnp.concatenate(vjs, axis=1)
        pv = [_dot(jnp.concatenate([vcat[h * HEAD_DIM:(h + 1) * HEAD_DIM], vcat[LANES:]], axis=0), ps[h]) for h in range(2)]
        for h in range(2):
            acc_sc[h] = alphas[h] * acc_sc[h] + pv[h]
        return carry

    lax.fori_loop(0, (qi + (G - 1)) // G, body, 0)
    a0 = acc_sc[0]
    a1 = acc_sc[1]
    o0 = a0[0:HEAD_DIM] / a0[HEAD_DIM:HEAD_DIM + 1]
    o1 = a1[0:HEAD_DIM] / a1[HEAD_DIM:HEAD_DIM + 1]
    o_ref[...] = jnp.concatenate([o0, o1], axis=0).T.astype(o_ref.dtype)


def moba_prompt(q, k, vt, kmean, *, batch, seq):
    tq = MOBA_BLOCK
    nq = seq // tq
    nb = seq // MOBA_BLOCK
    hp = D_GROUP // LANES
    q3 = q.reshape(batch, seq, D_GROUP)
    k3 = k.reshape(batch, seq, D_GROUP)
    km3 = kmean.reshape(batch, nb, D_GROUP)
    nd = LANES + L_ROWS
    out = pl.pallas_call(
        functools.partial(_moba_kernel, tq=tq, seq=seq),
        grid=(batch, hp, nq),
        in_specs=[pl.BlockSpec((None, tq, LANES), lambda b, p, i: (b, i, p)),
                  pl.BlockSpec((None, seq, LANES), lambda b, p, i: (b, 0, p)),
                  pl.BlockSpec((None, LANES, seq), lambda b, p, i: (b, p, 0)),
                  pl.BlockSpec((None, nb, LANES), lambda b, p, i: (b, 0, p))],
        out_specs=pl.BlockSpec((None, tq, LANES), lambda b, p, i: (b, i, p)),
        out_shape=jax.ShapeDtypeStruct((batch, seq, D_GROUP), BF16),
        scratch_shapes=[pltpu.VMEM((nd, seq), BF16),
                        pltpu.VMEM((2, LANES, tq), BF16), pltpu.VMEM((2, 1, tq), F32),
                        pltpu.VMEM((2, HEAD_DIM + L_ROWS, tq), F32), pltpu.VMEM((2 * MOBA_TOPK, 1, tq), I32)],
        compiler_params=_cparams(("arbitrary", "arbitrary", "arbitrary"), VMEM_LIMIT),
        name="moba_prompt",
    )(q3, k3, vt, km3)
    return out.reshape(batch * seq, D_GROUP)


DEC_PG = 16
DEC_NEW = 16
DEC_COLS = 32


def _moba_decode_kernel(pt_ref, q2_ref, kn_ref, vn_ref, *rest, n_steps, page, n_q):
    k_refs = rest[:DEC_PG]
    v_refs = rest[DEC_PG:2 * DEC_PG]
    o_ref = rest[2 * DEC_PG]
    s_sc, gate_sc, acc_sc, l_sc, pown_sc = rest[2 * DEC_PG + 1:]
    s = pl.program_id(1)
    ppb = MOBA_BLOCK // page
    nblk = n_steps * DEC_PG // ppb
    nc = DEC_COLS
    scale = HEAD_DIM ** -0.5
    q2 = q2_ref[...]
    lane = lax.broadcasted_iota(I32, (nc, LANES), 1)

    @pl.when(s == 0)
    def _():
        gate_sc[...] = jnp.zeros(gate_sc.shape, F32)

    @pl.when(s < n_steps)
    def _():
        g = gate_sc[...]
        for i in range(DEC_PG):
            pidx = s * DEC_PG + i
            sf = _dot(q2, k_refs[i][...].astype(BF16))
            s_sc[:, pl.ds(pl.multiple_of(pidx * page, page), page)] = sf
            g = g + jnp.where(lane == pidx // ppb, jnp.sum(sf, axis=1, keepdims=True), 0.0)
        gate_sc[...] = g

    @pl.when(s == n_steps - 1)
    def _():
        gate = jnp.where(lane < nblk, gate_sc[...] * (1.0 / MOBA_BLOCK), -jnp.inf)
        sels = []
        for r in range(MOBA_TOPK):
            mx = jnp.max(gate, axis=1, keepdims=True)
            ix = jnp.min(jnp.where(gate == mx, lane, LANES), axis=1, keepdims=True)
            gate = jnp.where(lane == ix, -jnp.inf, gate)
            sels.append(ix)
        so = _dot_nt(q2, kn_ref[...].astype(BF16)) * scale
        key_c = lax.broadcasted_iota(I32, so.shape, 1)
        q_r = lax.broadcasted_iota(I32, so.shape, 0) // N_HEADS
        own_ok = key_c <= q_r
        so = jnp.where(own_ok, so, NEG)
        npos = s_sc.shape[1]
        blk_of = lax.broadcasted_iota(I32, (nc, npos), 1) // MOBA_BLOCK
        mask = (blk_of == sels[0]) | (blk_of == sels[1]) | (blk_of == sels[2])
        sc = jnp.where(mask, s_sc[...] * scale, NEG)
        m = jnp.maximum(jnp.max(sc, axis=1, keepdims=True), jnp.max(so, axis=1, keepdims=True))
        p = jnp.where(mask, jnp.exp(sc - m), 0.0)
        po = jnp.where(own_ok, jnp.exp(so - m), 0.0)
        s_sc[...] = p
        pown_sc[...] = po
        l_sc[...] = jnp.sum(p, axis=1, keepdims=True) + jnp.sum(po, axis=1, keepdims=True)
        acc_sc[...] = jnp.zeros(acc_sc.shape, F32)

    @pl.when(s >= n_steps)
    def _():
        acc = acc_sc[...]
        for i in range(DEC_PG):
            pidx = (s - n_steps) * DEC_PG + i
            pj = s_sc[:, pl.ds(pl.multiple_of(pidx * page, page), page)].astype(BF16)
            acc = acc + _dot_nt(pj, v_refs[i][...].astype(BF16))
        acc_sc[...] = acc

    @pl.when(s == 2 * n_steps - 1)
    def _():
        acc = acc_sc[...] + _dot(pown_sc[...].astype(BF16), vn_ref[...].astype(BF16))
        out = acc / l_sc[...]
        hrow = lax.broadcasted_iota(I32, (N_HEADS, D_GROUP), 0)
        hlane = lax.broadcasted_iota(I32, (N_HEADS, D_GROUP), 1) // HEAD_DIM
        rows = []
        for q in range(n_q):
            blk8 = out[q * N_HEADS:(q + 1) * N_HEADS, :]
            rows.append(jnp.sum(jnp.where(hrow == hlane, blk8, 0.0), axis=0, keepdims=True))
        rows.append(jnp.zeros((8 - n_q, D_GROUP), F32))
        o_ref[...] = jnp.concatenate(rows, axis=0)


def moba_decode(page_table, q2, k_new, v_new, cache_kt, cache_vt, *, layer, n_q):
    bsz, n_pages = page_table.shape
    page = cache_kt.shape[3]
    assert n_pages % DEC_PG == 0 and MOBA_BLOCK % page == 0 and page % LANES == 0
    assert n_q <= 8 and n_q * N_HEADS <= DEC_COLS and MOBA_TOPK <= n_pages * page // MOBA_BLOCK <= LANES
    n_steps = n_pages // DEC_PG
    kspec = lambda i: pl.BlockSpec(
        (None, None, D_GROUP, page),
        lambda b, s, pt, i=i: (layer, pt[b, jnp.minimum(s, n_steps - 1) * DEC_PG + i], 0, 0))
    vspec = lambda i: pl.BlockSpec(
        (None, None, D_GROUP, page),
        lambda b, s, pt, i=i: (layer, pt[b, jnp.maximum(s - n_steps, 0) * DEC_PG + i], 0, 0))
    perb = lambda r, w: pl.BlockSpec((None, r, w), lambda b, s, pt: (b, 0, 0))
    return pl.pallas_call(
        functools.partial(_moba_decode_kernel, n_steps=n_steps, page=page, n_q=n_q),
        grid_spec=pltpu.PrefetchScalarGridSpec(
            num_scalar_prefetch=1,
            grid=(bsz, 2 * n_steps),
            in_specs=[perb(DEC_COLS, D_GROUP), perb(DEC_NEW, D_GROUP), perb(DEC_NEW, D_GROUP)]
            + [kspec(i) for i in range(DEC_PG)] + [vspec(i) for i in range(DEC_PG)],
            out_specs=perb(8, D_GROUP),
            scratch_shapes=[pltpu.VMEM((DEC_COLS, n_pages * page), F32), pltpu.VMEM((DEC_COLS, LANES), F32),
                            pltpu.VMEM((DEC_COLS, D_GROUP), F32), pltpu.VMEM((DEC_COLS, 1), F32),
                            pltpu.VMEM((DEC_COLS, DEC_NEW), F32)]),
        out_shape=jax.ShapeDtypeStruct((bsz, 8, D_GROUP), F32),
        compiler_params=_cparams(("arbitrary", "arbitrary"), VMEM_LIMIT),
        name="moba_decode",
    )(page_table, q2, k_new, v_new, *([cache_kt] * DEC_PG), *([cache_vt] * DEC_PG))


def _log_sigmoid(x):
    return jnp.minimum(x, 0.0) - jnp.log1p(jnp.exp(-jnp.abs(x)))


def _mlstm_kernel(qk_ref, v_ref, om_ref, g_ref, cw_ref, cb_ref, bg_ref, gh_ref,
                  conv0_ref, c0_ref, n0_ref, m0_ref,
                  h_ref, cout_ref, nout_ref, mout_ref,
                  conv_sc, c_sc, n_sc, m_sc, *, L, t_valid):
    ci = pl.program_id(1)
    nh, dh, dg = N_HEADS, HEAD_DIM, D_GROUP

    @pl.when(ci == 0)
    def _():
        conv_sc[...] = conv0_ref[...]
        c_sc[...] = c0_ref[...]
        n_sc[...] = n0_ref[...]
        m_sc[...] = m0_ref[...]

    x = qk_ref[...]
    xc = jnp.concatenate([conv_sc[...], x], axis=0)
    y = cb_ref[...]
    for j in range(CONV_W):
        off = 8 - (CONV_W - 1) + j
        y = y + xc[off:off + L] * cw_ref[j:j + 1, :]
    conv_sc[...] = x[L - 8:L]
    qk = y * jax.nn.sigmoid(y)
    q_all = qk[:, :dg].astype(BF16)
    k_all = qk[:, dg:] * (dh ** -0.5)
    v_all = v_ref[...]
    om = om_ref[...]

    g = g_ref[...] + bg_ref[...]
    lf = _log_sigmoid(g)
    ri = lax.broadcasted_iota(I32, (L, L), 0)
    cj = lax.broadcasted_iota(I32, (L, L), 1)
    causal = cj <= ri
    tril = jnp.where(causal, 1.0, 0.0).astype(BF16)
    lf3 = _split3(lf)
    b_col = _dot(tril, lf3[0]) + _dot(tril, lf3[1]) + _dot(tril, lf3[2])
    sel_r = lax.broadcasted_iota(I32, (8, LANES), 0)
    sel_c = lax.broadcasted_iota(I32, (8, LANES), 1)
    sel_i = jnp.where(sel_c == sel_r, 1.0, 0.0).astype(BF16)
    sel_f = jnp.where(sel_c == sel_r + nh, 1.0, 0.0).astype(BF16)

    def rows_of(sel, a):
        a3 = _split3(a)
        return _dot_nt(sel, a3[0]) + _dot_nt(sel, a3[1]) + _dot_nt(sel, a3[2])

    li_row = rows_of(sel_i, g)
    b_row = rows_of(sel_f, b_col)
    rvalid = lax.broadcasted_iota(I32, (L, 1), 0) < t_valid
    heads = range(nh)
    hsl = [slice(h * dh, (h + 1) * dh) for h in heads]
    qs = [q_all[:, hsl[h]] for h in heads]
    ks = [k_all[:, hsl[h]] for h in heads]
    vs = [v_all[:, hsl[h]] for h in heads]
    cs = [c_sc[h] for h in heads]
    ns = [n_sc[h:h + 1, :] for h in heads]
    qk_raw = [_dot_nt(qs[h], ks[h].astype(BF16)) for h in heads]
    qc = [_dot(qs[h], cs[h].astype(BF16)) for h in heads]
    bcs, m_prevs, m_ts, w_inters, ss = [], [], [], [], []
    for h in heads:
        bc = b_col[:, nh + h:nh + h + 1]
        log_d = jnp.where(causal, bc - b_row[h:h + 1, :] + li_row[h:h + 1, :], -jnp.inf)
        m_prev = m_sc[h:h + 1, 0:1]
        log_inter = m_prev + bc
        m_t = jnp.maximum(log_inter, jnp.max(log_d, axis=-1, keepdims=True))
        dw = jnp.exp(log_d - m_t)
        bcs.append(bc)
        m_prevs.append(m_prev)
        m_ts.append(m_t)
        w_inters.append(jnp.exp(log_inter - m_t))
        ss.append(qk_raw[h] * dw)
    sv = [_dot(ss[h].astype(BF16), vs[h]) for h in heads]
    kws, decays, m_news = [], [], []
    for h in heads:
        m_new = m_ts[h][t_valid - 1:t_valid, :]
        b_last = bcs[h][t_valid - 1:t_valid, :]
        w_state = jnp.where(rvalid, jnp.exp(b_last - bcs[h] + g[:, h:h + 1] - m_new), 0.0)
        decays.append(jnp.exp(m_prevs[h] + b_last - m_new))
        kws.append(ks[h] * w_state)
        m_news.append(m_new)
    kv = [lax.dot_general(kws[h].astype(BF16), vs[h], _TN, preferred_element_type=F32) for h in heads]
    outs = []
    for h in heads:
        num = sv[h] + w_inters[h] * qc[h]
        qn = jnp.sum(qs[h].astype(F32) * ns[h], axis=-1, keepdims=True)
        den = jnp.sum(ss[h], axis=-1, keepdims=True) + w_inters[h] * qn
        hh = num / jnp.maximum(jnp.abs(den), jnp.exp(-m_ts[h]))
        c_sc[h] = decays[h] * cs[h] + kv[h]
        n_sc[h:h + 1, :] = decays[h] * ns[h] + jnp.sum(kws[h], axis=0, keepdims=True)
        m_sc[h:h + 1, :] = jnp.broadcast_to(m_news[h], (1, LANES))
        hn = hh * lax.rsqrt(jnp.mean(hh * hh, axis=-1, keepdims=True) + EPS) * gh_ref[:, hsl[h]]
        outs.append(hn * jax.nn.sigmoid(om[:, hsl[h]]))
    h_ref[...] = jnp.concatenate(outs, axis=-1).astype(h_ref.dtype)
    cout_ref[...] = c_sc[...]
    nout_ref[...] = n_sc[...]
    mout_ref[...] = m_sc[...]


def mlstm(qk, vm, om, gates, conv_w, conv_b, bg_pad, g_head, conv0, c0, n0, m0b, *, L, t_valid):
    b, t, _ = qk.shape
    nh, dh, dg = N_HEADS, HEAD_DIM, D_GROUP
    nc = t // L
    tok = lambda w: pl.BlockSpec((None, L, w), lambda i, c: (i, c, 0))
    full2 = lambda r, w: pl.BlockSpec((r, w), lambda i, c: (0, 0))
    return pl.pallas_call(
        functools.partial(_mlstm_kernel, L=L, t_valid=t_valid),
        grid=(b, nc),
        in_specs=[tok(2 * dg), tok(dg), tok(dg), tok(GATE_PAD),
                  full2(CONV_W, 2 * dg), full2(1, 2 * dg), full2(1, GATE_PAD), full2(1, dg),
                  pl.BlockSpec((None, 8, 2 * dg), lambda i, c: (i, 0, 0)),
                  pl.BlockSpec((None, nh, dh, dh), lambda i, c: (i, 0, 0, 0)),
                  pl.BlockSpec((None, nh, dh), lambda i, c: (i, 0, 0)),
                  pl.BlockSpec((None, nh, LANES), lambda i, c: (i, 0, 0))],
        out_specs=[tok(dg),
                   pl.BlockSpec((None, nh, dh, dh), lambda i, c: (i, 0, 0, 0)),
                   pl.BlockSpec((None, nh, dh), lambda i, c: (i, 0, 0)),
                   pl.BlockSpec((None, nh, LANES), lambda i, c: (i, 0, 0))],
        out_shape=[jax.ShapeDtypeStruct((b, t, dg), BF16),
                   jax.ShapeDtypeStruct((b, nh, dh, dh), F32),
                   jax.ShapeDtypeStruct((b, nh, dh), F32),
                   jax.ShapeDtypeStruct((b, nh, LANES), F32)],
        scratch_shapes=[pltpu.VMEM((8, 2 * dg), F32), pltpu.VMEM((nh, dh, dh), F32),
                        pltpu.VMEM((nh, dh), F32), pltpu.VMEM((nh, LANES), F32)],
        compiler_params=_cparams(("arbitrary", "arbitrary"), VMEM_LIMIT),
        name="mlstm",
    )(qk, vm, om, gates, conv_w, conv_b, bg_pad, g_head, conv0, c0, n0, m0b)


def _outproj_kernel(att_ref, hm_ref, x_ref, gt_ref, sc_ref, sh_ref, g_ref, wo_ref, wrh_ref, br_ref, *rest):
    xo_ref, h2_ref, ti_ref, tw_ref = rest[-4:]
    dg = D_GROUP
    mix = _dot(att_ref[...], wo_ref[0:dg, :]) + _dot(hm_ref[...], wo_ref[dg:2 * dg, :])
    xn = x_ref[...] + gt_ref[...] * mix
    xo_ref[...] = xn
    ms = jnp.mean(xn * xn, axis=-1, keepdims=True)
    y = xn * lax.rsqrt(ms + EPS) * g_ref[...]
    h2 = y * (1.0 + sc_ref[...]) + sh_ref[...]
    hh = h2.astype(BF16)
    h2_ref[...] = hh.astype(F32)
    logits = _dot(hh, wrh_ref[...]) + br_ref[...]
    col = lax.broadcasted_iota(I32, logits.shape, 1)
    lg = jnp.where(col < N_EXPERTS, logits, -jnp.inf)
    vals, idxs = [], []
    for _ in range(TOP_K):
        mx = jnp.max(lg, axis=-1, keepdims=True)
        ix = jnp.min(jnp.where(lg == mx, col, LANES), axis=-1, keepdims=True)
        lg = jnp.where(col == ix, -jnp.inf, lg)
        vals.append(mx)
        idxs.append(ix)
    es = [jnp.exp(v - vals[0]) for v in vals]
    den = es[0] + es[1] + es[2] + es[3]
    ti = jnp.zeros(logits.shape, I32)
    tw = jnp.zeros(logits.shape, F32)
    for r in range(TOP_K):
        ti = jnp.where(col == r, idxs[r], ti)
        tw = jnp.where(col == r, es[r] / den, tw)
    ti_ref[...] = ti
    tw_ref[...] = tw


def out_proj(att, hm, x, mod, g_ffn, w_out, wr_hi, br_pad, routed, *, n_routed, row0, tm, tiles_per_group):
    n, d = x.shape
    rm = mod.shape[1]
    dg = D_GROUP
    b0 = row0 // tm
    row = lambda i: (i, 0)
    rrow = lambda i: (b0 + i, 0)
    modspec = lambda c: pl.BlockSpec((None, rm, d), lambda i, c=c: (i // tiles_per_group, 0, c))
    const = lambda r, w: pl.BlockSpec((r, w), lambda i: (0, 0))
    in_specs = [pl.BlockSpec((tm, dg), row), pl.BlockSpec((tm, dg), row), pl.BlockSpec((tm, d), row),
                modspec(2), modspec(4), modspec(3),
                const(1, d), const(2 * dg, d), const(d, LANES), const(1, LANES)]
    args = [att, hm, x, mod, mod, mod, g_ffn, w_out, wr_hi, br_pad]
    aliases = {}
    if routed is not None:
        aliases = {len(args) + j: 1 + j for j in range(3)}
        in_specs += [pl.BlockSpec(memory_space=pl.ANY)] * 3
        args += list(routed)
    return pl.pallas_call(
        _outproj_kernel,
        grid=(n // tm,),
        in_specs=in_specs,
        out_specs=[pl.BlockSpec((tm, d), row), pl.BlockSpec((tm, d), rrow),
                   pl.BlockSpec((tm, LANES), rrow), pl.BlockSpec((tm, LANES), rrow)],
        out_shape=[jax.ShapeDtypeStruct((n, d), F32), jax.ShapeDtypeStruct((n_routed, d), F32),
                   jax.ShapeDtypeStruct((n_routed, LANES), I32), jax.ShapeDtypeStruct((n_routed, LANES), F32)],
        input_output_aliases=aliases,
        compiler_params=_cparams(("arbitrary",), VMEM_LIMIT),
        name="out_proj",
    )(*args)


def _rank_kernel(ti_ref, rank_ref, cnt_ref, base_sc):
    i = pl.program_id(0)

    @pl.when(i == 0)
    def _():
        base_sc[...] = jnp.zeros(base_sc.shape, F32)

    ti = ti_ref[...]
    tm = ti.shape[0]
    col = lax.broadcasted_iota(I32, ti.shape, 1)
    es = []
    oh = jnp.zeros(ti.shape, F32)
    for k in range(TOP_K):
        ek = jnp.sum(jnp.where(col == k, ti, 0), axis=-1, keepdims=True)
        es.append(ek)
        oh = oh + jnp.where(col == ek, 1.0, 0.0)
    ri = lax.broadcasted_iota(I32, (tm, tm), 0)
    ci = lax.broadcasted_iota(I32, (tm, tm), 1)
    tri = jnp.where(ci < ri, 1.0, 0.0).astype(BF16)
    rank = _dot(tri, oh.astype(BF16)) + base_sc[0:1, :]
    out = jnp.zeros(ti.shape, F32)
    for k in range(TOP_K):
        rk = jnp.sum(jnp.where(col == es[k], rank, 0.0), axis=-1, keepdims=True)
        out = jnp.where(col == k, rk, out)
    rank_ref[...] = out.astype(I32)
    tot = base_sc[0:1, :] + jnp.sum(oh, axis=0, keepdims=True)
    base_sc[...] = jnp.broadcast_to(tot, base_sc.shape)
    cnt_ref[...] = base_sc[...].astype(I32)


def expert_ranks(ti, *, tm):
    n = ti.shape[0]
    return pl.pallas_call(
        _rank_kernel,
        grid=(n // tm,),
        in_specs=[pl.BlockSpec((tm, LANES), lambda i: (i, 0))],
        out_specs=[pl.BlockSpec((tm, LANES), lambda i: (i, 0)), pl.BlockSpec((8, LANES), lambda i: (0, 0))],
        out_shape=[jax.ShapeDtypeStruct((n, LANES), I32), jax.ShapeDtypeStruct((8, LANES), I32)],
        scratch_shapes=[pltpu.VMEM((8, LANES), F32)],
        compiler_params=_cparams(("arbitrary",)),
        name="expert_ranks",
    )(ti)


def _moe_kernel(be_ref, first_ref, nused_ref, x_ref, wgu_ref, bgu_ref, wd_ref, bd_ref, y_ref, wgu_sc, wd_sc):
    i = pl.program_id(0)
    f = wd_ref.shape[0]

    @pl.when(first_ref[i] == 1)
    def _():
        wgu_sc[...] = wgu_ref[...].astype(BF16)
        wd_sc[...] = wd_ref[...].astype(BF16)

    @pl.when(i < nused_ref[0])
    def _():
        gu = _dot(x_ref[...].astype(BF16), wgu_sc[...]) + bgu_ref[...]
        gg = jnp.minimum(gu[:, :f], SWIGLU_LIMIT)
        uu = jnp.clip(gu[:, f:], -SWIGLU_LIMIT, SWIGLU_LIMIT)
        act = (uu + 1.0) * gg * jax.nn.sigmoid(gg * SWIGLU_ALPHA)
        y_ref[...] = _dot(act.astype(BF16), wd_sc[...]) + bd_ref[...]


def moe_mlp(blk_exp, blk_first, n_used, xg, w_gu, b_gu, w_down, b_down, *, layer, bm):
    m, d = xg.shape
    depth, e, _, f2 = w_gu.shape
    f = f2 // 2
    nblk = m // bm
    xmap = lambda i, be, fi, nu: (jnp.minimum(i, nu[0] - 1), 0)
    wmap = lambda i, be, fi, nu: (layer, be[i], 0, 0)
    return pl.pallas_call(
        _moe_kernel,
        grid_spec=pltpu.PrefetchScalarGridSpec(
            num_scalar_prefetch=3,
            grid=(nblk,),
            in_specs=[pl.BlockSpec((bm, d), xmap),
                      pl.BlockSpec((None, None, d, f2), wmap),
                      pl.BlockSpec((None, None, 1, f2), wmap),
                      pl.BlockSpec((None, None, f, d), wmap),
                      pl.BlockSpec((None, None, 1, d), wmap)],
            out_specs=pl.BlockSpec((bm, d), xmap),
            scratch_shapes=[pltpu.VMEM((d, f2), BF16), pltpu.VMEM((f, d), BF16)]),
        out_shape=jax.ShapeDtypeStruct((m, d), F32),
        compiler_params=_cparams(("arbitrary",), VMEM_LIMIT),
        name="moe_mlp",
    )(blk_exp, blk_first, n_used, xg, w_gu, b_gu.reshape(depth, e, 1, f2), w_down, b_down.reshape(depth, e, 1, d))


def _combine_kernel(y_ref, tw_ref, x_ref, gt_ref, o_ref):
    tw = tw_ref[...]
    ff = y_ref[0] * tw[:, 0:1]
    for k in range(1, TOP_K):
        ff = ff + y_ref[k] * tw[:, k:k + 1]
    o_ref[...] = x_ref[...] + gt_ref[...] * ff


def moe_combine(yg, tw, x, mod, *, row0, tm, tiles_per_group):
    n, d = x.shape
    rm = mod.shape[1]
    b0 = row0 // tm
    return pl.pallas_call(
        _combine_kernel,
        grid=(n // tm,),
        in_specs=[pl.BlockSpec((TOP_K, tm, d), lambda i: (0, b0 + i, 0)),
                  pl.BlockSpec((tm, LANES), lambda i: (b0 + i, 0)),
                  pl.BlockSpec((tm, d), lambda i: (i, 0)),
                  pl.BlockSpec((None, rm, d), lambda i: (i // tiles_per_group, 0, N_ADA - 1))],
        out_specs=pl.BlockSpec((tm, d), lambda i: (i, 0)),
        out_shape=jax.ShapeDtypeStruct((n, d), F32),
        compiler_params=_cparams(("arbitrary",), VMEM_LIMIT),
        name="moe_combine",
    )(yg, tw, x, mod)


def _rms_kernel(x_ref, g_ref, o_ref):
    x = x_ref[...]
    o_ref[...] = x * lax.rsqrt(jnp.mean(x * x, axis=-1, keepdims=True) + EPS) * g_ref[...]


def final_norm(x, g, *, tm):
    n, d = x.shape
    return pl.pallas_call(
        _rms_kernel,
        grid=(n // tm,),
        in_specs=[pl.BlockSpec((tm, d), lambda i: (i, 0)), pl.BlockSpec((1, d), lambda i: (0, 0))],
        out_specs=pl.BlockSpec((tm, d), lambda i: (i, 0)),
        out_shape=jax.ShapeDtypeStruct((n, d), F32),
        compiler_params=_cparams(("arbitrary",)),
        name="final_norm",
    )(x, g)


def _decode_q2(q):
    b, ts, _ = q.shape
    q4 = q.reshape(b, ts, N_HEADS, HEAD_DIM)
    eye = jnp.eye(N_HEADS, dtype=F32)
    qbd = q4[:, :, None, :, :] * eye[None, None, :, :, None]
    qbd = qbd.reshape(b, ts * N_HEADS, D_GROUP)
    qbd = jnp.pad(qbd, ((0, 0), (0, DEC_COLS - ts * N_HEADS), (0, 0)))
    return qbd.astype(BF16)


def _route(ti, bm):
    n = ti.shape[0]
    na = n * TOP_K
    tm = max(t for t in range(8, 513, 8) if n % t == 0)
    rank, cnt = expert_ranks(ti, tm=tm)
    counts = cnt[0, :N_EXPERTS]
    padded = ((counts + bm - 1) // bm) * bm
    pad_end = jnp.cumsum(padded)
    pad_start = pad_end - padded
    e4 = ti[:, :TOP_K]
    onehot = e4[:, :, None] == jnp.arange(N_EXPERTS, dtype=I32)[None, None, :]
    pos = jnp.sum(jnp.where(onehot, pad_start[None, None, :], 0), axis=-1) + rank[:, :TOP_K]
    nblk = -(-(na + N_EXPERTS * (bm - 1)) // bm)
    m = nblk * bm
    tok = jnp.broadcast_to(jnp.arange(n, dtype=I32)[:, None], (n, TOP_K))
    row_tok = jnp.zeros((m,), I32).at[pos.reshape(-1)].set(tok.reshape(-1), unique_indices=True)
    blk_start = jnp.arange(nblk, dtype=I32) * bm
    blk_exp = jnp.minimum(jnp.sum(blk_start[:, None] >= pad_end[None, :], axis=1), N_EXPERTS - 1).astype(I32)
    n_used = (pad_end[-1] // bm).astype(I32)
    blk_first = jnp.concatenate([jnp.ones((1,), I32), (blk_exp[1:] != blk_exp[:-1]).astype(I32)])
    blk_first = jnp.where(jnp.arange(nblk) < n_used, blk_first, 0).astype(I32)
    return row_tok, blk_exp, blk_first, n_used.reshape(1), pos


def _layer(l, depth, xp, xs, modp, mods, P, S, W, stacks):
    dg = D_GROUP
    Bp, T = P["batch"], P["seq"]
    Bs, Ts = S["batch"], S["seq"]
    d = xp.shape[1]
    tm = 512
    kst_p, vst_p = stacks
    qa, kst_p, vst_p, kb, vbt, qk, vm, om, gt, kmean = in_proj(
        xp, modp, W["g_mix"][l], W["w_in"][l], kst_p, vst_p, layer=l, depth=depth,
        tm=tm, tiles_per_group=T // tm, fresh=True)
    att_p = moba_prompt(qa, kb, vbt, kmean.reshape(-1, dg), batch=Bp, seq=T)
    Lp = 256
    hm_p, c_p, n_p, m_p = mlstm(
        qk.reshape(Bp, T, 2 * dg), vm.reshape(Bp, T, dg), om.reshape(Bp, T, dg), gt.reshape(Bp, T, GATE_PAD),
        W["conv_w"][l], W["conv_b"][l], W["bg_pad"][l], W["g_head"][l],
        jnp.zeros((Bp, 8, 2 * dg), F32), jnp.zeros((Bp, N_HEADS, HEAD_DIM, HEAD_DIM), F32),
        jnp.zeros((Bp, N_HEADS, HEAD_DIM), F32), jnp.zeros((Bp, N_HEADS, LANES), F32), L=Lp, t_valid=Lp)
    conv_p = qk.reshape(Bp, T, 2 * dg)[:, T - (CONV_W - 1):]
    np_ = Bp * T
    ntot = np_ + Bs * Ts
    xp1, *routed = out_proj(att_p, hm_p.reshape(np_, dg), xp, modp, W["g_ffn"][l], W["w_out"][l],
                            W["wr_hi"][l], W["br_pad"][l], None,
                            n_routed=ntot, row0=0, tm=tm, tiles_per_group=T // tm)
    ns = Bs * Ts
    qa_s, ka_s, va_s, _, _, qk_s, vm_s, om_s, gt_s = in_proj(
        xs, mods, W["g_mix"][l], W["w_in"][l], None, None, layer=l, depth=depth,
        tm=ns, tiles_per_group=1, fresh=False)
    newpad = lambda a: jnp.pad(a.reshape(Bs, Ts, dg), ((0, 0), (0, DEC_NEW - Ts), (0, 0)))
    att_s = moba_decode(S["page_table"], _decode_q2(qa_s.reshape(Bs, Ts, dg)), newpad(ka_s), newpad(va_s),
                        S["cache_kt"], S["cache_vt"], layer=l, n_q=Ts)
    att_s = att_s[:, :Ts].reshape(ns, dg).astype(BF16)
    Ls = 16
    padt = lambda a: jnp.pad(a.reshape(Bs, Ts, -1), ((0, 0), (0, Ls - Ts), (0, 0)))
    conv0 = jnp.pad(S["state_conv"][l], ((0, 0), (8 - (CONV_W - 1), 0), (0, 0)))
    m0b = jnp.broadcast_to(S["state_m"][l][:, :, None], (Bs, N_HEADS, LANES))
    hm_s, c_s, n_s, m_s = mlstm(padt(qk_s), padt(vm_s), padt(om_s), padt(gt_s),
                                W["conv_w"][l], W["conv_b"][l], W["bg_pad"][l], W["g_head"][l],
                                conv0, S["state_c"][l], S["state_n"][l], m0b, L=Ls, t_valid=Ts)
    hm_s = hm_s[:, :Ts].reshape(ns, dg)
    conv_s = qk_s.reshape(Bs, Ts, 2 * dg)[:, Ts - (CONV_W - 1):]
    xs1, h2, ti, tw = out_proj(att_s, hm_s, xs, mods, W["g_ffn"][l], W["w_out"][l],
                               W["wr_hi"][l], W["br_pad"][l], routed,
                               n_routed=ntot, row0=np_, tm=ns, tiles_per_group=1)
    row_tok, blk_exp, blk_first, n_used, pos = _route(ti, MOE_BM)
    xg = h2[row_tok]
    yb = moe_mlp(blk_exp, blk_first, n_used, xg, W["w_gu"], W["b_gu"], W["w_down"], W["b_down"], layer=l, bm=MOE_BM)
    yg = yb[pos.T.reshape(-1)].reshape(TOP_K, ntot, d)
    xp2 = moe_combine(yg, tw, xp1, modp, row0=0, tm=tm, tiles_per_group=T // tm)
    xs2 = moe_combine(yg, tw, xs1, mods, row0=np_, tm=ns, tiles_per_group=1)
    outs_p = (c_p, n_p, m_p[:, :, 0], conv_p)
    outs_s = (ka_s.reshape(Bs, Ts, N_HEADS, HEAD_DIM), va_s.reshape(Bs, Ts, N_HEADS, HEAD_DIM),
              c_s, n_s, m_s[:, :, 0], conv_s)
    return xp2, xs2, outs_p, outs_s, (kst_p, vst_p)


def kernel(x_prompt, x_sample, c_prompt, c_sample, cache_k, cache_v, page_table, state_c, state_n, state_m,
           state_conv, w_ada, b_ada, g_mix, w_in, b_gate, conv_w, conv_b, g_head, w_out, g_ffn, w_router,
           b_router, w_gu, b_gu, w_down, b_down, g_final):
    depth = w_ada.shape[0]
    Bp, T, d = x_prompt.shape
    Bs, Ts, _ = x_sample.shape
    dg = D_GROUP
    n_gate = 2 * N_HEADS
    w_in_pad = jnp.pad(w_in, ((0, 0), (0, 0), (0, GATE_PAD - n_gate))).astype(BF16)
    wr_pad = jnp.pad(w_router, ((0, 0), (0, 0), (0, LANES - N_EXPERTS)))
    wr_hi = wr_pad.astype(BF16)
    W = dict(
        g_mix=g_mix.reshape(depth, 1, d), w_in=w_in_pad,
        conv_w=conv_w, conv_b=conv_b.reshape(depth, 1, 2 * dg),
        bg_pad=jnp.pad(b_gate, ((0, 0), (0, GATE_PAD - n_gate))).reshape(depth, 1, GATE_PAD),
        g_head=g_head.reshape(depth, 1, dg), w_out=w_out.astype(BF16), g_ffn=g_ffn.reshape(depth, 1, d),
        wr_hi=wr_hi,
        br_pad=jnp.pad(b_router, ((0, 0), (0, LANES - N_EXPERTS))).reshape(depth, 1, LANES),
        w_gu=w_gu, b_gu=b_gu, w_down=w_down, b_down=b_down)
    n_pool, page = cache_k.shape[1], cache_k.shape[2]
    assert (page_table.shape[1] * page) % MOBA_BLOCK == 0
    pages_t = lambda c: c.transpose(0, 1, 3, 4, 2).reshape(depth, n_pool, dg, page)
    S = dict(batch=Bs, seq=Ts, page_table=page_table, cache_kt=pages_t(cache_k), cache_vt=pages_t(cache_v),
             state_c=state_c, state_n=state_n, state_m=state_m, state_conv=state_conv)
    P = dict(batch=Bp, seq=T)
    nc = Bp + Bs
    rpad = -(-nc // 8) * 8
    c_all = jnp.pad(jnp.concatenate([c_prompt, c_sample], axis=0), ((0, rpad - nc), (0, 0)))
    mod = ada_mod(c_all, w_ada, b_ada)
    xp = x_prompt.reshape(Bp * T, d)
    xs = x_sample.reshape(Bs * Ts, d)
    acc_p, acc_s = [], []
    stacks = (None, None)
    for l in range(depth):
        modp = mod[l, :Bp].reshape(Bp, 1, N_ADA * d)
        mods = jnp.repeat(mod[l, Bp:nc], Ts, axis=0).reshape(1, Bs * Ts, N_ADA * d)
        xp, xs, op, os_, stacks = _layer(l, depth, xp, xs, modp, mods, P, S, W, stacks)
        acc_p.append(op)
        acc_s.append(os_)
    y_p = final_norm(xp, g_final.reshape(1, d), tm=512).reshape(Bp, T, d)
    y_s = final_norm(xs, g_final.reshape(1, d), tm=Bs * Ts).reshape(Bs, Ts, d)
    stack = lambda acc, i: jnp.stack([a[i] for a in acc])
    rows = lambda a: a.reshape(depth, Bp, N_HEADS, HEAD_DIM, T).transpose(0, 1, 4, 2, 3)
    return ((y_p, y_s, rows(stacks[0]), rows(stacks[1])) + tuple(stack(acc_p, i) for i in range(4))
            + tuple(stack(acc_s, i) for i in range(6)))
```
